```python
import jax, jax.numpy as jnp
from jax import lax
import numpy as np

D_MODEL = 1024
BATCH = 8
SEQ = 4096
DEPTH = 4

D_MIX = D_MODEL
N_GROUPS = 4
GROUP_WIDTH = D_MIX // N_GROUPS
HEAD_DIM = 64
N_HEADS_GROUP = GROUP_WIDTH // HEAD_DIM
CONF_KERNEL = 31
SHORT_KERNEL = 3
FFN_KERNEL = 3
D_FF = 2816
BLOCK_Q = 128
RET_CHUNK = 128
ROPE_BASE = 10000.0
NORM_EPS = 1e-6
N_IN_SLICES = 12
D_IN = N_IN_SLICES * GROUP_WIDTH

kernel_name = 'hybrid_parallel_heads_stickbreak_retention_conv'


def rms_norm(x, g):
    xf = x.astype(jnp.float32)
    y = xf * lax.rsqrt(jnp.mean(xf * xf, axis=-1, keepdims=True) + NORM_EPS)
    return (y * g.astype(jnp.float32)).astype(x.dtype)


def layer_norm(x, g, b):
    xf = x.astype(jnp.float32)
    mu = jnp.mean(xf, axis=-1, keepdims=True)
    xc = xf - mu
    y = xc * lax.rsqrt(jnp.mean(xc * xc, axis=-1, keepdims=True) + NORM_EPS)
    return (y * g.astype(jnp.float32) + b.astype(jnp.float32)).astype(x.dtype)


def causal_dwconv(x, w):
    k_w, c = w.shape
    return lax.conv_general_dilated(
        x, w[:, None, :].astype(x.dtype), window_strides=(1,), padding=[(k_w - 1, 0)],
        dimension_numbers=('NWC', 'WIO', 'NWC'), feature_group_count=c)


def split_heads(x):
    b, t, _ = x.shape
    return x.reshape(b, t, -1, HEAD_DIM).transpose(0, 2, 1, 3)


def merge_heads(x):
    b, h, t, d = x.shape
    return x.transpose(0, 2, 1, 3).reshape(b, t, h * d)


def rotary(x, pos):
    d = x.shape[-1]
    half = d // 2
    inv_freq = ROPE_BASE ** (-jnp.arange(0, d, 2, dtype=jnp.float32) / d)
    ang = pos.astype(jnp.float32)[:, None] * inv_freq[None, :]
    cos, sin = jnp.cos(ang), jnp.sin(ang)
    xf = x.astype(jnp.float32)
    x1, x2 = xf[..., :half], xf[..., half:]
    return jnp.concatenate([x1 * cos - x2 * sin, x2 * cos + x1 * sin], axis=-1)


def conformer_conv(a, gate, dw_w, dw_b, ln_g, ln_b):
    h = a * jax.nn.sigmoid(gate)
    h = causal_dwconv(h, dw_w) + dw_b.astype(h.dtype)
    return jax.nn.silu(layer_norm(h, ln_g, ln_b))


def stick_breaking_attention(q, k, v):
    b, h, t, d = q.shape
    n_blocks = t // BLOCK_Q
    scale = d ** -0.5
    kf = k.astype(jnp.float32)
    key_pos = jnp.arange(t)
    qb = q.reshape(b, h, n_blocks, BLOCK_Q, d).transpose(2, 0, 1, 3, 4)

    def block(args):
        qi, i = args
        z = jnp.einsum('bhqd,bhkd->bhqk', qi.astype(jnp.float32), kf) * scale
        q_pos = i * BLOCK_Q + jnp.arange(BLOCK_Q)
        mask = key_pos[None, :] < q_pos[:, None]
        log_1m_beta = jnp.where(mask, -jax.nn.softplus(z), 0.0)
        tail = lax.cumsum(log_1m_beta, axis=3, reverse=True) - log_1m_beta
        attn = jnp.where(mask, jnp.exp(jax.nn.log_sigmoid(z) + tail), 0.0)
        return jnp.einsum('bhqk,bhkd->bhqd', attn.astype(v.dtype), v)

    out = lax.map(block, (qb, jnp.arange(n_blocks)))
    return out.transpose(1, 2, 0, 3, 4).reshape(b, h, t, d)


def retention_chunkwise(q, k, v):
    b, h, t, d = q.shape
    n_chunks = t // RET_CHUNK
    log_gamma = jnp.log1p(-jnp.exp2(-5.0 - jnp.arange(h, dtype=jnp.float32)))
    idx = jnp.arange(RET_CHUNK, dtype=jnp.float32)
    diff = idx[:, None] - idx[None, :]
    causal = diff >= 0
    d_intra = jnp.where(causal[None], jnp.exp(jnp.where(causal, diff, 0.0)[None] * log_gamma[:, None, None]), 0.0)
    q_decay = jnp.exp((idx[None, :] + 1.0) * log_gamma[:, None])
    k_decay = jnp.exp((RET_CHUNK - 1.0 - idx[None, :]) * log_gamma[:, None])
    chunk_decay = jnp.exp(RET_CHUNK * log_gamma)
    k = k * (d ** -0.5)

    def to_chunks(z):
        return z.reshape(b, h, n_chunks, RET_CHUNK, d).transpose(2, 0, 1, 3, 4)

    def step(state, inp):
        qc, kc, vc = inp
        inner = jnp.einsum('bhqd,bhkd->bhqk', qc, kc) * d_intra
        o = jnp.einsum('bhqk,bhkd->bhqd', inner, vc) + jnp.einsum('bhqd,bhde->bhqe', qc, state) * q_decay[..., None]
        state = state * chunk_decay[:, None, None] + jnp.einsum('bhkd,bhke->bhde', kc * k_decay[..., None], vc)
        return state, o

    state0 = jnp.zeros((b, h, d, d), jnp.float32)
    _, out = lax.scan(step, state0, (to_chunks(q), to_chunks(k), to_chunks(v)))
    return out.transpose(1, 2, 0, 3, 4).reshape(b, h, t, d)


def retention_mixer(q, k, v, g, norm_g):
    t = q.shape[1]
    pos = jnp.arange(t)
    qh = rotary(split_heads(q), pos)
    kh = rotary(split_heads(k), pos)
    vh = split_heads(v).astype(jnp.float32)
    o = retention_chunkwise(qh, kh, vh)
    mu = jnp.mean(o, axis=-1, keepdims=True)
    oc = o - mu
    o = oc * lax.rsqrt(jnp.mean(oc * oc, axis=-1, keepdims=True) + NORM_EPS)
    y = merge_heads(o) * norm_g.astype(jnp.float32)
    return (jax.nn.silu(g.astype(jnp.float32)) * y).astype(q.dtype)


def short_gated_conv(b_gate, c_gate, h, conv_w):
    return b_gate * causal_dwconv(c_gate * h, conv_w)


def conv_ffn(x, w_up, w_conv, w_down):
    h = causal_dwconv(x @ w_up, w_conv)
    gate, up = jnp.split(h, 2, axis=-1)
    return (jax.nn.silu(gate) * up) @ w_down


def setup_inputs(seed: int = 0) -> dict:
    key = jax.random.key(seed)
    ks = jax.random.split(key, 16)
    f32 = jnp.float32

    def nrm(k, shape, scale):
        return jax.random.normal(k, shape, f32) * scale

    def gain(k, shape):
        return 1.0 + 0.05 * jax.random.normal(k, shape, f32)

    return {
        'x': nrm(ks[0], (BATCH, SEQ, D_MODEL), 1.0),
        'norm_mix_pre': gain(ks[1], (DEPTH, D_MODEL)),
        'norm_mix_post': gain(ks[2], (DEPTH, D_MODEL)),
        'norm_ffn_pre': gain(ks[3], (DEPTH, D_MODEL)),
        'norm_ffn_post': gain(ks[4], (DEPTH, D_MODEL)),
        'w_in': nrm(ks[5], (DEPTH, D_MODEL, D_IN), D_MODEL ** -0.5),
        'conf_dw_w': nrm(ks[6], (DEPTH, CONF_KERNEL, GROUP_WIDTH), CONF_KERNEL ** -0.5),
        'conf_dw_b': nrm(ks[7], (DEPTH, GROUP_WIDTH), 0.02),
        'conf_ln_g': gain(ks[8], (DEPTH, GROUP_WIDTH)),
        'conf_ln_b': nrm(ks[9], (DEPTH, GROUP_WIDTH), 0.02),
        'ret_norm_g': gain(ks[10], (DEPTH, GROUP_WIDTH)),
        'sc_conv_w': nrm(ks[11], (DEPTH, SHORT_KERNEL, GROUP_WIDTH), SHORT_KERNEL ** -0.5),
        'w_out': nrm(ks[12], (DEPTH, D_MIX, D_MODEL), D_MIX ** -0.5),
        'ffn_up': nrm(ks[13], (DEPTH, D_MODEL, 2 * D_FF), D_MODEL ** -0.5),
        'ffn_conv_w': nrm(ks[14], (DEPTH, FFN_KERNEL, 2 * D_FF), FFN_KERNEL ** -0.5),
        'ffn_down': nrm(ks[15], (DEPTH, D_FF, D_MODEL), D_FF ** -0.5),
    }


def reference(x, norm_mix_pre, norm_mix_post, norm_ffn_pre, norm_ffn_post, w_in,
              conf_dw_w, conf_dw_b, conf_ln_g, conf_ln_b, ret_norm_g, sc_conv_w,
              w_out, ffn_up, ffn_conv_w, ffn_down):
    for l in range(DEPTH):
        h = rms_norm(x, norm_mix_pre[l])
        proj = h @ w_in[l]
        (c_a, c_gate, sb_q, sb_k, sb_v, r_q, r_k, r_v, r_g,
         sc_b, sc_c, sc_h) = jnp.split(proj, N_IN_SLICES, axis=-1)
        y_conf = conformer_conv(c_a, c_gate, conf_dw_w[l], conf_dw_b[l], conf_ln_g[l], conf_ln_b[l])
        y_sb = merge_heads(stick_breaking_attention(split_heads(sb_q), split_heads(sb_k), split_heads(sb_v)))
        y_ret = retention_mixer(r_q, r_k, r_v, r_g, ret_norm_g[l])
        y_sc = short_gated_conv(sc_b, sc_c, sc_h, sc_conv_w[l])
        mix = jnp.concatenate([y_conf, y_sb, y_ret, y_sc], axis=-1) @ w_out[l]
        x = x + rms_norm(mix, norm_mix_post[l])
        f = conv_ffn(rms_norm(x, norm_ffn_pre[l]), ffn_up[l], ffn_conv_w[l], ffn_down[l])
        x = x + rms_norm(f, norm_ffn_post[l])
    return x
```

```python
import functools

import jax
import jax.numpy as jnp
from jax import lax
from jax.experimental import pallas as pl
from jax.experimental.pallas import tpu as pltpu

F32 = jnp.float32
BF16 = jnp.bfloat16

GROUP_WIDTH = 256
HEAD_DIM = 64
N_HEADS = GROUP_WIDTH // HEAD_DIM
CONF_KERNEL = 31
SHORT_KERNEL = 3
FFN_KERNEL = 3
RET_CHUNK = 128
ROPE_BASE = 10000.0
NORM_EPS = 1e-6

PF_CONF_A, PF_CONF_GATE, PF_RET_Q, PF_RET_K, PF_RET_V, PF_RET_G, PF_SC_B, PF_SC_C, PF_SC_H = range(9)
N_PF = 9

VMEM_LIMIT = 56 * 1024 * 1024

CONV_HALO = 32
SHORT_HALO = 8
FFN_HALO = 16
FFN_CHUNK = 256


def _dot(a, b):
    return jnp.dot(a, b, preferred_element_type=F32)


def _dot_nt(a, b):
    return lax.dot_general(a, b, (((1,), (1,)), ((), ())), preferred_element_type=F32)


def _split_bf16(x):
    hi = x.astype(BF16)
    lo = (x - hi.astype(F32)).astype(BF16)
    return hi, lo


def _resident(shape):
    nd = len(shape)
    return pl.BlockSpec(shape, lambda *_: (0,) * nd, pipeline_mode=pl.Buffered(1))


def _inproj_kernel(x_ref, g_ref, w_ref, pf_ref, qkv_ref):
    x = x_ref[...]
    ms = jnp.mean(x * x, axis=-1, keepdims=True)
    h = (x * lax.rsqrt(ms + NORM_EPS) * g_ref[...]).astype(BF16)
    gw = GROUP_WIDTH
    pf_ref[:, 0:2 * gw] = _dot(h, w_ref[:, 0:2 * gw])
    qkv_ref[:, 0:gw] = (_dot(h, w_ref[:, 2 * gw:3 * gw]) * (HEAD_DIM ** -0.5)).astype(BF16)
    qkv_ref[:, gw:3 * gw] = _dot(h, w_ref[:, 3 * gw:5 * gw]).astype(BF16)
    pf_ref[:, 2 * gw:N_PF * gw] = _dot(h, w_ref[:, 5 * gw:12 * gw])


def _inproj(x2, g, w, tm):
    n, d = x2.shape
    d_in = w.shape[1]
    return pl.pallas_call(
        _inproj_kernel,
        grid=(n // tm,),
        in_specs=[
            pl.BlockSpec((tm, d), lambda i: (i, 0)),
            _resident((1, d)),
            _resident((d, d_in)),
        ],
        out_specs=[
            pl.BlockSpec((tm, N_PF * GROUP_WIDTH), lambda i: (i, 0)),
            pl.BlockSpec((tm, 3 * GROUP_WIDTH), lambda i: (i, 0)),
        ],
        out_shape=[
            jax.ShapeDtypeStruct((n, N_PF * GROUP_WIDTH), F32),
            jax.ShapeDtypeStruct((n, 3 * GROUP_WIDTH), BF16),
        ],
        compiler_params=pltpu.CompilerParams(
            dimension_semantics=("parallel",), vmem_limit_bytes=VMEM_LIMIT),
        name="inproj",
    )(x2, g, w)


def _convmix_kernel(a_ref, gate_ref, a_h_ref, gate_h_ref, scb_ref, scc_ref, sch_ref,
                    scc_h_ref, sch_h_ref, dww_ref, dwb_ref, lng_ref, lnb_ref, scw_ref,
                    yconf_ref, ysc_ref, hbuf, ubuf, *, tt, sub):
    t = pl.program_id(1)
    first = t == 0

    halo = a_h_ref[0] * jax.nn.sigmoid(gate_h_ref[0])
    hbuf[0:CONV_HALO, :] = jnp.where(first, 0.0, halo)
    hbuf[CONV_HALO:CONV_HALO + tt, :] = a_ref[0] * jax.nn.sigmoid(gate_ref[0])

    uhalo = scc_h_ref[0] * sch_h_ref[0]
    ubuf[0:SHORT_HALO, :] = jnp.where(first, 0.0, uhalo)
    ubuf[SHORT_HALO:SHORT_HALO + tt, :] = scc_ref[0] * sch_ref[0]

    bias = dwb_ref[...]
    lng = lng_ref[...]
    lnb = lnb_ref[...]
    for r in range(tt // sub):
        base = r * sub
        acc = jnp.broadcast_to(bias, (sub, GROUP_WIDTH))
        for k in range(CONF_KERNEL):
            off = CONV_HALO - (CONF_KERNEL - 1) + k + base
            acc = acc + dww_ref[k:k + 1, :] * hbuf[off:off + sub, :]
        mu = jnp.mean(acc, axis=-1, keepdims=True)
        xc = acc - mu
        var = jnp.mean(xc * xc, axis=-1, keepdims=True)
        y = xc * lax.rsqrt(var + NORM_EPS) * lng + lnb
        yconf_ref[0, base:base + sub, :] = (y * jax.nn.sigmoid(y)).astype(yconf_ref.dtype)

        conv = jnp.zeros((sub, GROUP_WIDTH), F32)
        for k in range(SHORT_KERNEL):
            off = SHORT_HALO - (SHORT_KERNEL - 1) + k + base
            conv = conv + scw_ref[k:k + 1, :] * ubuf[off:off + sub, :]
        ysc_ref[0, base:base + sub, :] = (scb_ref[0, base:base + sub, :] * conv).astype(ysc_ref.dtype)


def _convmix(pf3, dww, dwb, lng, lnb, scw, tt):
    b, t, _ = pf3.shape
    gw = GROUP_WIDTH
    sub = 64

    def main(col):
        return pl.BlockSpec((1, tt, gw), lambda bi, ti: (bi, ti, col))

    def halo(col, rows):
        per = tt // rows
        return pl.BlockSpec((1, rows, gw), lambda bi, ti: (bi, jnp.maximum(ti * per - 1, 0), col))

    kern = functools.partial(_convmix_kernel, tt=tt, sub=sub)
    return pl.pallas_call(
        kern,
        grid=(b, t // tt),
        in_specs=[
            main(PF_CONF_A), main(PF_CONF_GATE),
            halo(PF_CONF_A, CONV_HALO), halo(PF_CONF_GATE, CONV_HALO),
            main(PF_SC_B), main(PF_SC_C), main(PF_SC_H),
            halo(PF_SC_C, SHORT_HALO), halo(PF_SC_H, SHORT_HALO),
            _resident((CONF_KERNEL, gw)), _resident((1, gw)), _resident((1, gw)),
            _resident((1, gw)), _resident((SHORT_KERNEL, gw)),
        ],
        out_specs=[
            pl.BlockSpec((1, tt, gw), lambda bi, ti: (bi, ti, 0)),
            pl.BlockSpec((1, tt, gw), lambda bi, ti: (bi, ti, 0)),
        ],
        out_shape=[
            jax.ShapeDtypeStruct((b, t, gw), BF16),
            jax.ShapeDtypeStruct((b, t, gw), BF16),
        ],
        scratch_shapes=[
            pltpu.VMEM((CONV_HALO + tt, gw), F32),
            pltpu.VMEM((SHORT_HALO + tt, gw), F32),
        ],
        compiler_params=pltpu.CompilerParams(
            dimension_semantics=("parallel", "arbitrary"), vmem_limit_bytes=VMEM_LIMIT),
        name="convmix",
    )(pf3, pf3, pf3, pf3, pf3, pf3, pf3, pf3, pf3, dww, dwb, lng, lnb, scw)


def _sb_kernel(q_ref, k_ref, v_ref, o_ref, acc_ref, *, tq):
    i = pl.program_id(1)
    lane_head = lax.broadcasted_iota(jnp.int32, (1, GROUP_WIDTH), 1) // HEAD_DIM
    row = lax.broadcasted_iota(jnp.int32, (tq, tq), 0)
    col = lax.broadcasted_iota(jnp.int32, (tq, tq), 1)
    later = jnp.where(row > col, 1.0, 0.0).astype(BF16)
    causal = col < row

    acc_ref[...] = jnp.zeros_like(acc_ref)
    q = q_ref[0]

    def head_body(h, _):
        in_head = lane_head == h
        qh = jnp.where(in_head, q, jnp.zeros_like(q))

        def block(j, run, diag):
            start = pl.multiple_of(j * tq, tq)
            kj = k_ref[0, pl.ds(start, tq), :]
            vj = v_ref[0, pl.ds(start, tq), :]
            z = _dot_nt(qh, kj)
            lg = jnp.log(1.0 + jnp.exp(-jnp.abs(z)))
            sp = jnp.maximum(z, 0.0) + lg
            if diag:
                sp = jnp.where(causal, sp, 0.0)
            hi, lo = _split_bf16(sp)
            after = _dot(hi, later) + _dot(lo, later)
            logit = jnp.minimum(z, 0.0) - lg - after - run
            p = jnp.exp(logit)
            if diag:
                p = jnp.where(causal, p, 0.0)
            vh = jnp.where(in_head, vj, jnp.zeros_like(vj))
            acc_ref[...] += _dot(p.astype(BF16), vh)
            return run + jnp.sum(sp, axis=1, keepdims=True)

        run = block(i, jnp.zeros((tq, 1), F32), True)
        lax.fori_loop(0, i, lambda jj, c: block(i - 1 - jj, c, False), run)
        return 0

    lax.fori_loop(0, N_HEADS, head_body, 0)
    o_ref[0] = acc_ref[...].astype(o_ref.dtype)


def _stickbreak(qkv3, tq):
    b, t, _ = qkv3.shape
    gw = GROUP_WIDTH
    kern = functools.partial(_sb_kernel, tq=tq)
    return pl.pallas_call(
        kern,
        grid=(b, t // tq),
        in_specs=[
            pl.BlockSpec((1, tq, gw), lambda bi, qi: (bi, qi, 0)),
            pl.BlockSpec((1, t, gw), lambda bi, qi: (bi, 0, 1)),
            pl.BlockSpec((1, t, gw), lambda bi, qi: (bi, 0, 2)),
        ],
        out_specs=pl.BlockSpec((1, tq, gw), lambda bi, qi: (bi, qi, 0)),
        out_shape=jax.ShapeDtypeStruct((b, t, gw), BF16),
        scratch_shapes=[pltpu.VMEM((tq, gw), F32)],
        compiler_params=pltpu.CompilerParams(
            dimension_semantics=("parallel", "arbitrary"), vmem_limit_bytes=VMEM_LIMIT),
        name="stickbreak",
    )(qkv3, qkv3, qkv3)


def _ret_kernel(q_ref, k_ref, v_ref, g_ref, cos_ref, sin_ref, dintra_ref, qdec_ref, kdec_ref,
                cdec_ref, ng_ref, o_ref, state_ref, *, tt):
    gw = GROUP_WIDTH
    c = RET_CHUNK

    @pl.when(pl.program_id(1) == 0)
    def _():
        state_ref[...] = jnp.zeros_like(state_ref)

    lane = lax.broadcasted_iota(jnp.int32, (1, gw), 1)
    lane_head = lane // HEAD_DIM
    first_half = (lane % HEAD_DIM) < (HEAD_DIM // 2)
    rr = lax.broadcasted_iota(jnp.int32, (gw, gw), 0) // HEAD_DIM
    cc = lax.broadcasted_iota(jnp.int32, (gw, gw), 1) // HEAD_DIM
    same_head = rr == cc
    head_mean = jnp.where(same_head, 1.0 / HEAD_DIM, 0.0).astype(BF16)

    def rot(x, cos, sin):
        swapped = jnp.where(first_half, pltpu.roll(x, gw - HEAD_DIM // 2, 1),
                            pltpu.roll(x, HEAD_DIM // 2, 1))
        return x * cos + swapped * sin

    def seg_mean(x):
        hi, lo = _split_bf16(x)
        return _dot(hi, head_mean) + _dot(lo, head_mean)

    for ci in range(tt // c):
        rows = slice(ci * c, (ci + 1) * c)
        cos = cos_ref[rows, :]
        sin = sin_ref[rows, :]
        qr = rot(q_ref[0, rows, :], cos, sin)
        kr = rot(k_ref[0, rows, :], cos, sin) * (HEAD_DIM ** -0.5)
        v = v_ref[0, rows, :]
        kb = kr.astype(BF16)
        vb = v.astype(BF16)
        state = state_ref[...]

        o = _dot(qr.astype(BF16), state.astype(BF16)) * qdec_ref[...]
        for h in range(N_HEADS):
            in_head = lane_head == h
            qh = jnp.where(in_head, qr, 0.0).astype(BF16)
            inner = _dot_nt(qh, kb) * dintra_ref[h]
            vh = jnp.where(in_head, vb, jnp.zeros_like(vb))
            o = o + _dot(inner.astype(BF16), vh)

        kd_t = (kr * kdec_ref[...]).T.astype(BF16)
        kv = _dot(kd_t, vb)
        state_ref[...] = state * cdec_ref[...] + jnp.where(same_head, kv, 0.0)

        mu = seg_mean(o)
        oc = o - mu
        var = seg_mean(oc * oc)
        y = oc * lax.rsqrt(var + NORM_EPS) * ng_ref[...]
        g = g_ref[0, rows, :]
        o_ref[0, rows, :] = (g * jax.nn.sigmoid(g) * y).astype(o_ref.dtype)


def _retention_tables(t):
    d = HEAD_DIM
    half = d // 2
    inv_freq = ROPE_BASE ** (-jnp.arange(0, d, 2, dtype=F32) / d)
    ang = jnp.arange(t).astype(F32)[:, None] * inv_freq[None, :]
    cos, sin = jnp.cos(ang), jnp.sin(ang)
    cos_h = jnp.concatenate([cos, cos], axis=-1)
    sin_h = jnp.concatenate([-sin, sin], axis=-1)
    cos_t = jnp.tile(cos_h, (1, N_HEADS))
    sin_t = jnp.tile(sin_h, (1, N_HEADS))

    log_gamma = jnp.log1p(-jnp.exp2(-5.0 - jnp.arange(N_HEADS, dtype=F32)))
    idx = jnp.arange(RET_CHUNK, dtype=F32)
    diff = idx[:, None] - idx[None, :]
    causal = diff >= 0
    d_intra = jnp.where(causal[None], jnp.exp(jnp.where(causal, diff, 0.0)[None] * log_gamma[:, None, None]), 0.0)
    q_decay = jnp.exp((idx[None, :] + 1.0) * log_gamma[:, None])
    k_decay = jnp.exp((RET_CHUNK - 1.0 - idx[None, :]) * log_gamma[:, None])
    chunk_decay = jnp.exp(RET_CHUNK * log_gamma)
    qdec = jnp.repeat(q_decay.T, d, axis=1)
    kdec = jnp.repeat(k_decay.T, d, axis=1)
    cdec = jnp.repeat(chunk_decay, d)[None, :]
    return cos_t, sin_t, d_intra, qdec, kdec, cdec


def _retention(pf3, tables, norm_g, tt):
    b, t, _ = pf3.shape
    gw = GROUP_WIDTH
    cos_t, sin_t, d_intra, qdec, kdec, cdec = tables

    def main(col):
        return pl.BlockSpec((1, tt, gw), lambda bi, ti: (bi, ti, col))

    kern = functools.partial(_ret_kernel, tt=tt)
    return pl.pallas_call(
        kern,
        grid=(b, t // tt),
        in_specs=[
            main(PF_RET_Q), main(PF_RET_K), main(PF_RET_V), main(PF_RET_G),
            pl.BlockSpec((tt, gw), lambda bi, ti: (ti, 0)),
            pl.BlockSpec((tt, gw), lambda bi, ti: (ti, 0)),
            _resident((N_HEADS, RET_CHUNK, RET_CHUNK)),
            _resident((RET_CHUNK, gw)), _resident((RET_CHUNK, gw)),
            _resident((1, gw)), _resident((1, gw)),
        ],
        out_specs=pl.BlockSpec((1, tt, gw), lambda bi, ti: (bi, ti, 0)),
        out_shape=jax.ShapeDtypeStruct((b, t, gw), BF16),
        scratch_shapes=[pltpu.VMEM((gw, gw), F32)],
        compiler_params=pltpu.CompilerParams(
            dimension_semantics=("parallel", "arbitrary"), vmem_limit_bytes=VMEM_LIMIT),
        name="retention",
    )(pf3, pf3, pf3, pf3, cos_t, sin_t, d_intra, qdec, kdec, cdec, norm_g)


def _outproj_kernel(yc_ref, ysb_ref, yr_ref, ysc_ref, w_ref, g_ref, x_ref, o_ref):
    gw = GROUP_WIDTH
    mix = _dot(yc_ref[...], w_ref[0:gw, :])
    mix = mix + _dot(ysb_ref[...], w_ref[gw:2 * gw, :])
    mix = mix + _dot(yr_ref[...], w_ref[2 * gw:3 * gw, :])
    mix = mix + _dot(ysc_ref[...], w_ref[3 * gw:4 * gw, :])
    ms = jnp.mean(mix * mix, axis=-1, keepdims=True)
    o_ref[...] = x_ref[...] + mix * lax.rsqrt(ms + NORM_EPS) * g_ref[...]


def _outproj(ys, w, g, x2, tm):
    n, d = x2.shape
    gw = GROUP_WIDTH
    yspec = pl.BlockSpec((tm, gw), lambda i: (i, 0))
    return pl.pallas_call(
        _outproj_kernel,
        grid=(n // tm,),
        in_specs=[yspec, yspec, yspec, yspec, _resident(w.shape), _resident((1, d)),
                  pl.BlockSpec((tm, d), lambda i: (i, 0))],
        out_specs=pl.BlockSpec((tm, d), lambda i: (i, 0)),
        out_shape=jax.ShapeDtypeStruct((n, d), F32),
        compiler_params=pltpu.CompilerParams(
            dimension_semantics=("parallel",), vmem_limit_bytes=VMEM_LIMIT),
        name="outproj",
    )(*ys, w, g, x2)


def _ffn_kernel(x_ref, xh_ref, gpre_ref, gpost_ref, wup_ref, wconv_ref, wdown_ref, o_ref,
                xn_ref, hg_ref, hu_ref, f_ref, *, tm, tiles_per_seq, n_chunks):
    i = pl.program_id(0)
    seq_start = (i % tiles_per_seq) == 0
    gpre = gpre_ref[...]

    def norm(x):
        ms = jnp.mean(x * x, axis=-1, keepdims=True)
        return x * lax.rsqrt(ms + NORM_EPS) * gpre

    x = x_ref[...]
    xn_ref[0:FFN_HALO, :] = jnp.where(seq_start, 0.0, norm(xh_ref[...])).astype(BF16)
    xn_ref[FFN_HALO:FFN_HALO + tm, :] = norm(x).astype(BF16)
    f_ref[...] = jnp.zeros_like(f_ref)

    def conv(h_ref, w):
        out = None
        for k in range(FFN_KERNEL):
            off = FFN_HALO - (FFN_KERNEL - 1) + k
            term = w[k:k + 1, :] * h_ref[off:off + tm, :]
            out = term if out is None else out + term
        return out

    def chunk(c, _):
        xn = xn_ref[...]
        hg_ref[...] = _dot(xn, wup_ref[c])
        hu_ref[...] = _dot(xn, wup_ref[n_chunks + c])
        gate = conv(hg_ref, wconv_ref[c])
        up = conv(hu_ref, wconv_ref[n_chunks + c])
        act = (gate * jax.nn.sigmoid(gate) * up).astype(BF16)
        f_ref[...] += _dot(act, wdown_ref[c])
        return 0

    lax.fori_loop(0, n_chunks, chunk, 0)
    f = f_ref[...]
    ms = jnp.mean(f * f, axis=-1, keepdims=True)
    o_ref[...] = x + f * lax.rsqrt(ms + NORM_EPS) * gpost_ref[...]


def _ffn(x2, gpre, gpost, wup, wconv, wdown, tm, seq):
    n, d = x2.shape
    n_chunks = wdown.shape[0]
    fc = wdown.shape[1]
    per = tm // FFN_HALO
    kern = functools.partial(_ffn_kernel, tm=tm, tiles_per_seq=seq // tm, n_chunks=n_chunks)
    return pl.pallas_call(
        kern,
        grid=(n // tm,),
        in_specs=[
            pl.BlockSpec((tm, d), lambda i: (i, 0)),
            pl.BlockSpec((FFN_HALO, d), lambda i: (jnp.maximum(i * per - 1, 0), 0)),
            _resident((1, d)), _resident((1, d)),
            _resident(wup.shape), _resident(wconv.shape), _resident(wdown.shape),
        ],
        out_specs=pl.BlockSpec((tm, d), lambda i: (i, 0)),
        out_shape=jax.ShapeDtypeStruct((n, d), F32),
        scratch_shapes=[
            pltpu.VMEM((FFN_HALO + tm, d), BF16),
            pltpu.VMEM((FFN_HALO + tm, fc), F32),
            pltpu.VMEM((FFN_HALO + tm, fc), F32),
            pltpu.VMEM((tm, d), F32),
        ],
        compiler_params=pltpu.CompilerParams(
            dimension_semantics=("parallel",), vmem_limit_bytes=VMEM_LIMIT),
        name="convffn",
    )(x2, x2, gpre, gpost, wup, wconv, wdown)


def kernel(x, norm_mix_pre, norm_mix_post, norm_ffn_pre, norm_ffn_post, w_in, conf_dw_w, conf_dw_b, conf_ln_g, conf_ln_b, ret_norm_g, sc_conv_w, w_out, ffn_up, ffn_conv_w, ffn_down):
    b, t, d = x.shape
    depth = w_in.shape[0]
    d_ff = ffn_down.shape[1]
    n = b * t
    gw = GROUP_WIDTH
    assert w_in.shape[2] == 12 * gw and w_out.shape[1] == 4 * gw
    assert d_ff % FFN_CHUNK == 0
    n_chunks = d_ff // FFN_CHUNK

    tm = min(512, t)
    tt = min(512, t)
    tq = min(256, t)
    assert t % tm == 0 and t % tq == 0 and t % RET_CHUNK == 0

    tables = _retention_tables(t)
    x2 = x.reshape(n, d)
    for l in range(depth):
        row = lambda a: a[l][None, :]
        w_in_l = w_in[l].astype(BF16)
        w_out_l = w_out[l].astype(BF16)
        wup_l = ffn_up[l].astype(BF16).reshape(d, 2 * n_chunks, FFN_CHUNK).transpose(1, 0, 2)
        wconv_l = ffn_conv_w[l].reshape(FFN_KERNEL, 2 * n_chunks, FFN_CHUNK).transpose(1, 0, 2)
        wdown_l = ffn_down[l].astype(BF16).reshape(n_chunks, FFN_CHUNK, d)

        pf, qkv = _inproj(x2, row(norm_mix_pre), w_in_l, tm)
        pf3 = pf.reshape(b, t, N_PF * gw)
        qkv3 = qkv.reshape(b, t, 3 * gw)
        y_conf, y_sc = _convmix(pf3, conf_dw_w[l], row(conf_dw_b), row(conf_ln_g), row(conf_ln_b),
                                sc_conv_w[l], tt)
        y_sb = _stickbreak(qkv3, tq)
        y_ret = _retention(pf3, tables, row(ret_norm_g), tt)
        ys = [y.reshape(n, gw) for y in (y_conf, y_sb, y_ret, y_sc)]
        x2 = _outproj(ys, w_out_l, row(norm_mix_post), x2, tm)
        x2 = _ffn(x2, row(norm_ffn_pre), row(norm_ffn_post), wup_l, wconv_l, wdown_l, tm, t)
    return x2.reshape(b, t, d)
```

```python
import functools

import jax
import jax.numpy as jnp
from jax import lax
from jax.experimental import pallas as pl
from jax.experimental.pallas import tpu as pltpu

F32 = jnp.float32
BF16 = jnp.bfloat16

GROUP_WIDTH = 256
HEAD_DIM = 64
N_HEADS = GROUP_WIDTH // HEAD_DIM
CONF_KERNEL = 31
SHORT_KERNEL = 3
FFN_KERNEL = 3
RET_CHUNK = 128
ROPE_BASE = 10000.0
NORM_EPS = 1e-6
LOG2E = 1.4426950408889634

PF_CONF_A, PF_CONF_GATE, PF_RET_Q, PF_RET_K, PF_RET_V, PF_RET_G, PF_SC_B, PF_SC_C, PF_SC_H = range(9)
N_PF = 9

VMEM_LIMIT = 56 * 1024 * 1024

CONV_HALO = 32
SHORT_HALO = 8
FFN_HALO = 16
FFN_CHUNK = 256


def _dot(a, b):
    return jnp.dot(a, b, preferred_element_type=F32)


def _dot_nt(a, b):
    return lax.dot_general(a, b, (((1,), (1,)), ((), ())), preferred_element_type=F32)


def _split_bf16(x):
    hi = x.astype(BF16)
    lo = (x - hi.astype(F32)).astype(BF16)
    return hi, lo


def _resident(shape):
    nd = len(shape)
    return pl.BlockSpec(shape, lambda *_: (0,) * nd, pipeline_mode=pl.Buffered(1))


def _inproj_kernel(x_ref, g_ref, w_ref, pf_ref, qkv_ref):
    x = x_ref[...]
    ms = jnp.mean(x * x, axis=-1, keepdims=True)
    h = (x * lax.rsqrt(ms + NORM_EPS) * g_ref[...]).astype(BF16)
    gw = GROUP_WIDTH
    pf_ref[:, 0:2 * gw] = _dot(h, w_ref[:, 0:2 * gw])
    qkv_ref[:, 0:gw] = (_dot(h, w_ref[:, 2 * gw:3 * gw]) * (HEAD_DIM ** -0.5)).astype(BF16)
    qkv_ref[:, gw:3 * gw] = _dot(h, w_ref[:, 3 * gw:5 * gw]).astype(BF16)
    pf_ref[:, 2 * gw:N_PF * gw] = _dot(h, w_ref[:, 5 * gw:12 * gw])


def _inproj(x2, g, w, tm):
    n, d = x2.shape
    d_in = w.shape[1]
    return pl.pallas_call(
        _inproj_kernel,
        grid=(n // tm,),
        in_specs=[
            pl.BlockSpec((tm, d), lambda i: (i, 0)),
            _resident((1, d)),
            _resident((d, d_in)),
        ],
        out_specs=[
            pl.BlockSpec((tm, N_PF * GROUP_WIDTH), lambda i: (i, 0)),
            pl.BlockSpec((tm, 3 * GROUP_WIDTH), lambda i: (i, 0)),
        ],
        out_shape=[
            jax.ShapeDtypeStruct((n, N_PF * GROUP_WIDTH), F32),
            jax.ShapeDtypeStruct((n, 3 * GROUP_WIDTH), BF16),
        ],
        compiler_params=pltpu.CompilerParams(
            dimension_semantics=("parallel",), vmem_limit_bytes=VMEM_LIMIT),
        name="inproj",
    )(x2, g, w)


def _convmix_kernel(a_ref, gate_ref, a_h_ref, gate_h_ref, scb_ref, scc_ref, sch_ref,
                    scc_h_ref, sch_h_ref, dww_ref, dwb_ref, lng_ref, lnb_ref, scw_ref,
                    yconf_ref, ysc_ref, hbuf, ubuf, *, tt, sub):
    t = pl.program_id(1)
    first = t == 0

    halo = a_h_ref[0] * jax.nn.sigmoid(gate_h_ref[0])
    hbuf[0:CONV_HALO, :] = jnp.where(first, 0.0, halo)
    hbuf[CONV_HALO:CONV_HALO + tt, :] = a_ref[0] * jax.nn.sigmoid(gate_ref[0])

    uhalo = scc_h_ref[0] * sch_h_ref[0]
    ubuf[0:SHORT_HALO, :] = jnp.where(first, 0.0, uhalo)
    ubuf[SHORT_HALO:SHORT_HALO + tt, :] = scc_ref[0] * sch_ref[0]

    bias = dwb_ref[...]
    lng = lng_ref[...]
    lnb = lnb_ref[...]
    for r in range(tt // sub):
        base = r * sub
        acc = jnp.broadcast_to(bias, (sub, GROUP_WIDTH))
        for k in range(CONF_KERNEL):
            off = CONV_HALO - (CONF_KERNEL - 1) + k + base
            acc = acc + dww_ref[k:k + 1, :] * hbuf[off:off + sub, :]
        mu = jnp.mean(acc, axis=-1, keepdims=True)
        xc = acc - mu
        var = jnp.mean(xc * xc, axis=-1, keepdims=True)
        y = xc * lax.rsqrt(var + NORM_EPS) * lng + lnb
        yconf_ref[0, base:base + sub, :] = (y * jax.nn.sigmoid(y)).astype(yconf_ref.dtype)

        conv = jnp.zeros((sub, GROUP_WIDTH), F32)
        for k in range(SHORT_KERNEL):
            off = SHORT_HALO - (SHORT_KERNEL - 1) + k + base
            conv = conv + scw_ref[k:k + 1, :] * ubuf[off:off + sub, :]
        ysc_ref[0, base:base + sub, :] = (scb_ref[0, base:base + sub, :] * conv).astype(ysc_ref.dtype)


def _convmix(pf3, dww, dwb, lng, lnb, scw, tt):
    b, t, _ = pf3.shape
    gw = GROUP_WIDTH
    sub = 64

    def main(col):
        return pl.BlockSpec((1, tt, gw), lambda bi, ti: (bi, ti, col))

    def halo(col, rows):
        per = tt // rows
        return pl.BlockSpec((1, rows, gw), lambda bi, ti: (bi, jnp.maximum(ti * per - 1, 0), col))

    kern = functools.partial(_convmix_kernel, tt=tt, sub=sub)
    return pl.pallas_call(
        kern,
        grid=(b, t // tt),
        in_specs=[
            main(PF_CONF_A), main(PF_CONF_GATE),
            halo(PF_CONF_A, CONV_HALO), halo(PF_CONF_GATE, CONV_HALO),
            main(PF_SC_B), main(PF_SC_C), main(PF_SC_H),
            halo(PF_SC_C, SHORT_HALO), halo(PF_SC_H, SHORT_HALO),
            _resident((CONF_KERNEL, gw)), _resident((1, gw)), _resident((1, gw)),
            _resident((1, gw)), _resident((SHORT_KERNEL, gw)),
        ],
        out_specs=[
            pl.BlockSpec((1, tt, gw), lambda bi, ti: (bi, ti, 0)),
            pl.BlockSpec((1, tt, gw), lambda bi, ti: (bi, ti, 0)),
        ],
        out_shape=[
            jax.ShapeDtypeStruct((b, t, gw), BF16),
            jax.ShapeDtypeStruct((b, t, gw), BF16),
        ],
        scratch_shapes=[
            pltpu.VMEM((CONV_HALO + tt, gw), F32),
            pltpu.VMEM((SHORT_HALO + tt, gw), F32),
        ],
        compiler_params=pltpu.CompilerParams(
            dimension_semantics=("parallel", "arbitrary"), vmem_limit_bytes=VMEM_LIMIT),
        name="convmix",
    )(pf3, pf3, pf3, pf3, pf3, pf3, pf3, pf3, pf3, dww, dwb, lng, lnb, scw)


def _sb_kernel(q_ref, k_ref, v_ref, o_ref, acc_ref, *, tq):
    i = pl.program_id(1)
    lane_head = lax.broadcasted_iota(jnp.int32, (1, GROUP_WIDTH), 1) // HEAD_DIM
    row = lax.broadcasted_iota(jnp.int32, (tq, tq), 0)
    col = lax.broadcasted_iota(jnp.int32, (tq, tq), 1)
    from_key = jnp.where(row >= col, 1.0, 0.0).astype(BF16)
    causal = col < row

    from2 = jnp.concatenate([from_key, from_key], axis=0)

    acc_ref[...] = jnp.zeros_like(acc_ref)
    q = q_ref[0]
    in_head = [lane_head == h for h in range(N_HEADS)]
    qh = [jnp.where(m, q, jnp.zeros_like(q)) for m in in_head]

    def block(j, runs, diag):
        start = pl.multiple_of(j * tq, tq)
        kj = k_ref[0, pl.ds(start, tq), :]
        vj = v_ref[0, pl.ds(start, tq), :]
        ps, new_runs = [], []
        for h in range(N_HEADS):
            z = _dot_nt(qh[h], kj)
            lg = jnp.log(1.0 + jnp.exp2(jnp.abs(z) * (-LOG2E)))
            sp = jnp.maximum(z, 0.0) + lg
            if diag:
                sp = jnp.where(causal, sp, 0.0)
            hi, lo = _split_bf16(sp)
            tail = _dot(jnp.concatenate([hi, lo], axis=1), from2)
            p = jnp.exp(z - tail - runs[h])
            if diag:
                p = jnp.where(causal, p, 0.0)
            ps.append(p.astype(BF16))
            new_runs.append(runs[h] + jnp.sum(sp, axis=1, keepdims=True))
        v_heads = jnp.concatenate([jnp.where(m, vj, jnp.zeros_like(vj)) for m in in_head], axis=0)
        acc_ref[...] += _dot(jnp.concatenate(ps, axis=1), v_heads)
        return tuple(new_runs)

    zero = jnp.zeros((tq, 1), F32)
    runs = block(i, (zero,) * N_HEADS, True)

    def pair(jj, c):
        j = i - 1 - 2 * jj
        return block(j - 1, block(j, c, False), False)

    runs = lax.fori_loop(0, i // 2, pair, runs)

    @pl.when(i % 2 == 1)
    def _():
        block(0, runs, False)

    o_ref[0] = acc_ref[...].astype(o_ref.dtype)


def _stickbreak(qkv3, tq):
    b, t, _ = qkv3.shape
    gw = GROUP_WIDTH
    kern = functools.partial(_sb_kernel, tq=tq)
    return pl.pallas_call(
        kern,
        grid=(b, t // tq),
        in_specs=[
            pl.BlockSpec((1, tq, gw), lambda bi, qi: (bi, qi, 0)),
            pl.BlockSpec((1, t, gw), lambda bi, qi: (bi, 0, 1)),
            pl.BlockSpec((1, t, gw), lambda bi, qi: (bi, 0, 2)),
        ],
        out_specs=pl.BlockSpec((1, tq, gw), lambda bi, qi: (bi, qi, 0)),
        out_shape=jax.ShapeDtypeStruct((b, t, gw), BF16),
        scratch_shapes=[pltpu.VMEM((tq, gw), F32)],
        compiler_params=pltpu.CompilerParams(
            dimension_semantics=("parallel", "arbitrary"), vmem_limit_bytes=VMEM_LIMIT),
        name="stickbreak",
    )(qkv3, qkv3, qkv3)


def _ret_kernel(q_ref, k_ref, v_ref, g_ref, cos_ref, sin_ref, dintra_ref, qdec_ref, kdec_ref,
                cdec_ref, ng_ref, o_ref, state_ref, *, tt):
    gw = GROUP_WIDTH
    c = RET_CHUNK

    @pl.when(pl.program_id(1) == 0)
    def _():
        state_ref[...] = jnp.zeros_like(state_ref)

    lane = lax.broadcasted_iota(jnp.int32, (1, gw), 1)
    lane_head = lane // HEAD_DIM
    first_half = (lane % HEAD_DIM) < (HEAD_DIM // 2)
    rr = lax.broadcasted_iota(jnp.int32, (gw, gw), 0) // HEAD_DIM
    cc = lax.broadcasted_iota(jnp.int32, (gw, gw), 1) // HEAD_DIM
    same_head = rr == cc
    head_mean = jnp.where(same_head, 1.0 / HEAD_DIM, 0.0).astype(BF16)

    def rot(x, cos, sin):
        swapped = jnp.where(first_half, pltpu.roll(x, gw - HEAD_DIM // 2, 1),
                            pltpu.roll(x, HEAD_DIM // 2, 1))
        return x * cos + swapped * sin

    def seg_mean(x):
        hi, lo = _split_bf16(x)
        return _dot(hi, head_mean) + _dot(lo, head_mean)

    for ci in range(tt // c):
        rows = slice(ci * c, (ci + 1) * c)
        cos = cos_ref[rows, :]
        sin = sin_ref[rows, :]
        qr = rot(q_ref[0, rows, :], cos, sin)
        kr = rot(k_ref[0, rows, :], cos, sin) * (HEAD_DIM ** -0.5)
        v = v_ref[0, rows, :]
        kb = kr.astype(BF16)
        vb = v.astype(BF16)
        state = state_ref[...]

        o = _dot(qr.astype(BF16), state.astype(BF16)) * qdec_ref[...]
        for h in range(N_HEADS):
            in_head = lane_head == h
            qh = jnp.where(in_head, qr, 0.0).astype(BF16)
            inner = _dot_nt(qh, kb) * dintra_ref[h]
            vh = jnp.where(in_head, vb, jnp.zeros_like(vb))
            o = o + _dot(inner.astype(BF16), vh)

        kd_t = (kr * kdec_ref[...]).T.astype(BF16)
        kv = _dot(kd_t, vb)
        state_ref[...] = state * cdec_ref[...] + jnp.where(same_head, kv, 0.0)

        mu = seg_mean(o)
        oc = o - mu
        var = seg_mean(oc * oc)
        y = oc * lax.rsqrt(var + NORM_EPS) * ng_ref[...]
        g = g_ref[0, rows, :]
        o_ref[0, rows, :] = (g * jax.nn.sigmoid(g) * y).astype(o_ref.dtype)


def _retention_tables(t):
    d = HEAD_DIM
    half = d // 2
    inv_freq = ROPE_BASE ** (-jnp.arange(0, d, 2, dtype=F32) / d)
    ang = jnp.arange(t).astype(F32)[:, None] * inv_freq[None, :]
    cos, sin = jnp.cos(ang), jnp.sin(ang)
    cos_h = jnp.concatenate([cos, cos], axis=-1)
    sin_h = jnp.concatenate([-sin, sin], axis=-1)
    cos_t = jnp.tile(cos_h, (1, N_HEADS))
    sin_t = jnp.tile(sin_h, (1, N_HEADS))

    log_gamma = jnp.log1p(-jnp.exp2(-5.0 - jnp.arange(N_HEADS, dtype=F32)))
    idx = jnp.arange(RET_CHUNK, dtype=F32)
    diff = idx[:, None] - idx[None, :]
    causal = diff >= 0
    d_intra = jnp.where(causal[None], jnp.exp(jnp.where(causal, diff, 0.0)[None] * log_gamma[:, None, None]), 0.0)
    q_decay = jnp.exp((idx[None, :] + 1.0) * log_gamma[:, None])
    k_decay = jnp.exp((RET_CHUNK - 1.0 - idx[None, :]) * log_gamma[:, None])
    chunk_decay = jnp.exp(RET_CHUNK * log_gamma)
    qdec = jnp.repeat(q_decay.T, d, axis=1)
    kdec = jnp.repeat(k_decay.T, d, axis=1)
    cdec = jnp.repeat(chunk_decay, d)[None, :]
    return cos_t, sin_t, d_intra, qdec, kdec, cdec


def _retention(pf3, tables, norm_g, tt):
    b, t, _ = pf3.shape
    gw = GROUP_WIDTH
    cos_t, sin_t, d_intra, qdec, kdec, cdec = tables

    def main(col):
        return pl.BlockSpec((1, tt, gw), lambda bi, ti: (bi, ti, col))

    kern = functools.partial(_ret_kernel, tt=tt)
    return pl.pallas_call(
        kern,
        grid=(b, t // tt),
        in_specs=[
            main(PF_RET_Q), main(PF_RET_K), main(PF_RET_V), main(PF_RET_G),
            pl.BlockSpec((tt, gw), lambda bi, ti: (ti, 0)),
            pl.BlockSpec((tt, gw), lambda bi, ti: (ti, 0)),
            _resident((N_HEADS, RET_CHUNK, RET_CHUNK)),
            _resident((RET_CHUNK, gw)), _resident((RET_CHUNK, gw)),
            _resident((1, gw)), _resident((1, gw)),
        ],
        out_specs=pl.BlockSpec((1, tt, gw), lambda bi, ti: (bi, ti, 0)),
        out_shape=jax.ShapeDtypeStruct((b, t, gw), BF16),
        scratch_shapes=[pltpu.VMEM((gw, gw), F32)],
        compiler_params=pltpu.CompilerParams(
            dimension_semantics=("parallel", "arbitrary"), vmem_limit_bytes=VMEM_LIMIT),
        name="retention",
    )(pf3, pf3, pf3, pf3, cos_t, sin_t, d_intra, qdec, kdec, cdec, norm_g)


def _outproj_kernel(yc_ref, ysb_ref, yr_ref, ysc_ref, w_ref, g_ref, x_ref, o_ref):
    gw = GROUP_WIDTH
    mix = _dot(yc_ref[...], w_ref[0:gw, :])
    mix = mix + _dot(ysb_ref[...], w_ref[gw:2 * gw, :])
    mix = mix + _dot(yr_ref[...], w_ref[2 * gw:3 * gw, :])
    mix = mix + _dot(ysc_ref[...], w_ref[3 * gw:4 * gw, :])
    ms = jnp.mean(mix * mix, axis=-1, keepdims=True)
    o_ref[...] = x_ref[...] + mix * lax.rsqrt(ms + NORM_EPS) * g_ref[...]


def _outproj(ys, w, g, x2, tm):
    n, d = x2.shape
    gw = GROUP_WIDTH
    yspec = pl.BlockSpec((tm, gw), lambda i: (i, 0))
    return pl.pallas_call(
        _outproj_kernel,
        grid=(n // tm,),
        in_specs=[yspec, yspec, yspec, yspec, _resident(w.shape), _resident((1, d)),
                  pl.BlockSpec((tm, d), lambda i: (i, 0))],
        out_specs=pl.BlockSpec((tm, d), lambda i: (i, 0)),
        out_shape=jax.ShapeDtypeStruct((n, d), F32),
        compiler_params=pltpu.CompilerParams(
            dimension_semantics=("parallel",), vmem_limit_bytes=VMEM_LIMIT),
        name="outproj",
    )(*ys, w, g, x2)


def _ffn_kernel(x_ref, xh_ref, gpre_ref, gpost_ref, wup_ref, wconv_ref, wdown_ref, o_ref,
                xn_ref, hga_ref, hua_ref, hgb_ref, hub_ref, f_ref, *, tm, tiles_per_seq, n_chunks):
    i = pl.program_id(0)
    seq_start = (i % tiles_per_seq) == 0
    gpre = gpre_ref[...]

    def norm(x):
        ms = jnp.mean(x * x, axis=-1, keepdims=True)
        return x * lax.rsqrt(ms + NORM_EPS) * gpre

    x = x_ref[...]
    xn_ref[0:FFN_HALO, :] = jnp.where(seq_start, 0.0, norm(xh_ref[...])).astype(BF16)
    xn_ref[FFN_HALO:FFN_HALO + tm, :] = norm(x).astype(BF16)
    f_ref[...] = jnp.zeros_like(f_ref)

    def conv(h_ref, w):
        out = None
        for k in range(FFN_KERNEL):
            off = FFN_HALO - (FFN_KERNEL - 1) + k
            term = w[k:k + 1, :] * h_ref[off:off + tm, :]
            out = term if out is None else out + term
        return out

    def up_proj(c, h_refs):
        xn = xn_ref[...]
        h_refs[0][...] = _dot(xn, wup_ref[c])
        h_refs[1][...] = _dot(xn, wup_ref[n_chunks + c])

    def down_proj(c, h_refs):
        gate = conv(h_refs[0], wconv_ref[c])
        up = conv(h_refs[1], wconv_ref[n_chunks + c])
        act = (gate * jax.nn.sigmoid(gate) * up).astype(BF16)
        f_ref[...] += _dot(act, wdown_ref[c])

    slot_a = (hga_ref, hua_ref)
    slot_b = (hgb_ref, hub_ref)
    assert n_chunks % 2 == 1
    up_proj(0, slot_a)

    def chunk_pair(cp, _):
        c = 2 * cp
        up_proj(c + 1, slot_b)
        down_proj(c, slot_a)
        up_proj(c + 2, slot_a)
        down_proj(c + 1, slot_b)
        return 0

    lax.fori_loop(0, n_chunks // 2, chunk_pair, 0)
    down_proj(n_chunks - 1, slot_a)
    f = f_ref[...]
    ms = jnp.mean(f * f, axis=-1, keepdims=True)
    o_ref[...] = x + f * lax.rsqrt(ms + NORM_EPS) * gpost_ref[...]


def _ffn(x2, gpre, gpost, wup, wconv, wdown, tm, seq):
    n, d = x2.shape
    n_chunks = wdown.shape[0]
    fc = wdown.shape[1]
    per = tm // FFN_HALO
    kern = functools.partial(_ffn_kernel, tm=tm, tiles_per_seq=seq // tm, n_chunks=n_chunks)
    return pl.pallas_call(
        kern,
        grid=(n // tm,),
        in_specs=[
            pl.BlockSpec((tm, d), lambda i: (i, 0)),
            pl.BlockSpec((FFN_HALO, d), lambda i: (jnp.maximum(i * per - 1, 0), 0)),
            _resident((1, d)), _resident((1, d)),
            _resident(wup.shape), _resident(wconv.shape), _resident(wdown.shape),
        ],
        out_specs=pl.BlockSpec((tm, d), lambda i: (i, 0)),
        out_shape=jax.ShapeDtypeStruct((n, d), F32),
        scratch_shapes=[
            pltpu.VMEM((FFN_HALO + tm, d), BF16),
            pltpu.VMEM((FFN_HALO + tm, fc), F32),
            pltpu.VMEM((FFN_HALO + tm, fc), F32),
            pltpu.VMEM((FFN_HALO + tm, fc), F32),
            pltpu.VMEM((FFN_HALO + tm, fc), F32),
            pltpu.VMEM((tm, d), F32),
        ],
        compiler_params=pltpu.CompilerParams(
            dimension_semantics=("parallel",), vmem_limit_bytes=VMEM_LIMIT),
        name="convffn",
    )(x2, x2, gpre, gpost, wup, wconv, wdown)


def kernel(x, norm_mix_pre, norm_mix_post, norm_ffn_pre, norm_ffn_post, w_in, conf_dw_w, conf_dw_b, conf_ln_g, conf_ln_b, ret_norm_g, sc_conv_w, w_out, ffn_up, ffn_conv_w, ffn_down):
    b, t, d = x.shape
    depth = w_in.shape[0]
    d_ff = ffn_down.shape[1]
    n = b * t
    gw = GROUP_WIDTH
    assert w_in.shape[2] == 12 * gw and w_out.shape[1] == 4 * gw
    assert d_ff % FFN_CHUNK == 0
    n_chunks = d_ff // FFN_CHUNK

    tm = min(512, t)
    tt = min(512, t)
    tq = min(256, t)
    assert t % tm == 0 and t % tq == 0 and t % RET_CHUNK == 0

    tables = _retention_tables(t)
    x2 = x.reshape(n, d)
    for l in range(depth):
        row = lambda a: a[l][None, :]
        w_in_l = w_in[l].astype(BF16)
        w_out_l = w_out[l].astype(BF16)
        wup_l = ffn_up[l].astype(BF16).reshape(d, 2 * n_chunks, FFN_CHUNK).transpose(1, 0, 2)
        wconv_l = ffn_conv_w[l].reshape(FFN_KERNEL, 2 * n_chunks, FFN_CHUNK).transpose(1, 0, 2)
        wdown_l = ffn_down[l].astype(BF16).reshape(n_chunks, FFN_CHUNK, d)

        pf, qkv = _inproj(x2, row(norm_mix_pre), w_in_l, tm)
        pf3 = pf.reshape(b, t, N_PF * gw)
        qkv3 = qkv.reshape(b, t, 3 * gw)
        y_conf, y_sc = _convmix(pf3, conf_dw_w[l], row(conf_dw_b), row(conf_ln_g), row(conf_ln_b),
                                sc_conv_w[l], tt)
        y_sb = _stickbreak(qkv3, tq)
        y_ret = _retention(pf3, tables, row(ret_norm_g), tt)
        ys = [y.reshape(n, gw) for y in (y_conf, y_sb, y_ret, y_sc)]
        x2 = _outproj(ys, w_out_l, row(norm_mix_post), x2, tm)
        x2 = _ffn(x2, row(norm_ffn_pre), row(norm_ffn_post), wup_l, wconv_l, wdown_l, tm, t)
    return x2.reshape(b, t, d)
```

```python
import functools

import jax
import jax.numpy as jnp
from jax import lax
from jax.experimental import pallas as pl
from jax.experimental.pallas import tpu as pltpu

F32 = jnp.float32
BF16 = jnp.bfloat16

GROUP_WIDTH = 256
HEAD_DIM = 64
N_HEADS = GROUP_WIDTH // HEAD_DIM
CONF_KERNEL = 31
SHORT_KERNEL = 3
FFN_KERNEL = 3
RET_CHUNK = 128
ROPE_BASE = 10000.0
NORM_EPS = 1e-6
LOG2E = 1.4426950408889634

PF_CONF_A, PF_CONF_GATE, PF_RET_Q, PF_RET_K, PF_RET_V, PF_RET_G, PF_SC_B, PF_SC_C, PF_SC_H = range(9)
N_PF = 9

VMEM_LIMIT = 56 * 1024 * 1024
F32_SUBLANES = 8

CONV_HALO = 32
SHORT_HALO = 8
FFN_HALO = 16
FFN_CHUNK = 256


def _dot(a, b):
    return jnp.dot(a, b, preferred_element_type=F32)


def _dot_nt(a, b):
    return lax.dot_general(a, b, (((1,), (1,)), ((), ())), preferred_element_type=F32)


def _split_bf16(x):
    hi = x.astype(BF16)
    lo = (x - hi.astype(F32)).astype(BF16)
    return hi, lo


def _resident(shape):
    nd = len(shape)
    return pl.BlockSpec(shape, lambda *_: (0,) * nd, pipeline_mode=pl.Buffered(1))


def _inproj_kernel(x_ref, g_ref, w_ref, pf_ref, qkv_ref):
    x = x_ref[...]
    ms = jnp.mean(x * x, axis=-1, keepdims=True)
    h = (x * lax.rsqrt(ms + NORM_EPS) * g_ref[...]).astype(BF16)
    gw = GROUP_WIDTH
    pf_ref[:, 0:2 * gw] = _dot(h, w_ref[:, 0:2 * gw])
    qkv_ref[:, 0:gw] = (_dot(h, w_ref[:, 2 * gw:3 * gw]) * (HEAD_DIM ** -0.5)).astype(BF16)
    qkv_ref[:, gw:3 * gw] = _dot(h, w_ref[:, 3 * gw:5 * gw]).astype(BF16)
    pf_ref[:, 2 * gw:N_PF * gw] = _dot(h, w_ref[:, 5 * gw:12 * gw])


def _inproj(x2, g, w, tm):
    n, d = x2.shape
    d_in = w.shape[1]
    return pl.pallas_call(
        _inproj_kernel,
        grid=(n // tm,),
        in_specs=[
            pl.BlockSpec((tm, d), lambda i: (i, 0)),
            _resident((1, d)),
            _resident((d, d_in)),
        ],
        out_specs=[
            pl.BlockSpec((tm, N_PF * GROUP_WIDTH), lambda i: (i, 0)),
            pl.BlockSpec((tm, 3 * GROUP_WIDTH), lambda i: (i, 0)),
        ],
        out_shape=[
            jax.ShapeDtypeStruct((n, N_PF * GROUP_WIDTH), F32),
            jax.ShapeDtypeStruct((n, 3 * GROUP_WIDTH), BF16),
        ],
        compiler_params=pltpu.CompilerParams(
            dimension_semantics=("parallel",), vmem_limit_bytes=VMEM_LIMIT),
        name="inproj",
    )(x2, g, w)


def _convmix_kernel(a_ref, gate_ref, a_h_ref, gate_h_ref, scb_ref, scc_ref, sch_ref,
                    scc_h_ref, sch_h_ref, dww_ref, dwb_ref, lng_ref, lnb_ref, scw_ref,
                    yconf_ref, ysc_ref, hbuf, ubuf, hrot, *, tt, sub):
    t = pl.program_id(1)
    first = t == 0

    halo = a_h_ref[0] * jax.nn.sigmoid(gate_h_ref[0])
    hbuf[0:CONV_HALO, :] = jnp.where(first, 0.0, halo)
    hbuf[CONV_HALO:CONV_HALO + tt, :] = a_ref[0] * jax.nn.sigmoid(gate_ref[0])

    uhalo = scc_h_ref[0] * sch_h_ref[0]
    ubuf[0:SHORT_HALO, :] = jnp.where(first, 0.0, uhalo)
    ubuf[SHORT_HALO:SHORT_HALO + tt, :] = scc_ref[0] * sch_ref[0]

    n_rot = hrot.shape[1]
    for s in range(1, F32_SUBLANES):
        hrot[s - 1, :, :] = hbuf[s:s + n_rot, :]

    bias = dwb_ref[...]
    lng = lng_ref[...]
    lnb = lnb_ref[...]
    for r in range(tt // sub):
        base = r * sub
        acc = jnp.broadcast_to(bias, (sub, GROUP_WIDTH))
        for k in range(CONF_KERNEL):
            off = CONV_HALO - (CONF_KERNEL - 1) + k
            phase = off % F32_SUBLANES
            lo = off - phase + base
            if phase == 0:
                taps = hbuf[lo:lo + sub, :]
            else:
                taps = hrot[phase - 1, lo:lo + sub, :]
            acc = acc + dww_ref[k:k + 1, :] * taps
        mu = jnp.mean(acc, axis=-1, keepdims=True)
        xc = acc - mu
        var = jnp.mean(xc * xc, axis=-1, keepdims=True)
        y = xc * lax.rsqrt(var + NORM_EPS) * lng + lnb
        yconf_ref[0, base:base + sub, :] = (y * jax.nn.sigmoid(y)).astype(yconf_ref.dtype)

        conv = jnp.zeros((sub, GROUP_WIDTH), F32)
        for k in range(SHORT_KERNEL):
            off = SHORT_HALO - (SHORT_KERNEL - 1) + k + base
            conv = conv + scw_ref[k:k + 1, :] * ubuf[off:off + sub, :]
        ysc_ref[0, base:base + sub, :] = (scb_ref[0, base:base + sub, :] * conv).astype(ysc_ref.dtype)


def _convmix(pf3, dww, dwb, lng, lnb, scw, tt):
    b, t, _ = pf3.shape
    gw = GROUP_WIDTH
    sub = 64

    def main(col):
        return pl.BlockSpec((1, tt, gw), lambda bi, ti: (bi, ti, col))

    def halo(col, rows):
        per = tt // rows
        return pl.BlockSpec((1, rows, gw), lambda bi, ti: (bi, jnp.maximum(ti * per - 1, 0), col))

    kern = functools.partial(_convmix_kernel, tt=tt, sub=sub)
    return pl.pallas_call(
        kern,
        grid=(b, t // tt),
        in_specs=[
            main(PF_CONF_A), main(PF_CONF_GATE),
            halo(PF_CONF_A, CONV_HALO), halo(PF_CONF_GATE, CONV_HALO),
            main(PF_SC_B), main(PF_SC_C), main(PF_SC_H),
            halo(PF_SC_C, SHORT_HALO), halo(PF_SC_H, SHORT_HALO),
            _resident((CONF_KERNEL, gw)), _resident((1, gw)), _resident((1, gw)),
            _resident((1, gw)), _resident((SHORT_KERNEL, gw)),
        ],
        out_specs=[
            pl.BlockSpec((1, tt, gw), lambda bi, ti: (bi, ti, 0)),
            pl.BlockSpec((1, tt, gw), lambda bi, ti: (bi, ti, 0)),
        ],
        out_shape=[
            jax.ShapeDtypeStruct((b, t, gw), BF16),
            jax.ShapeDtypeStruct((b, t, gw), BF16),
        ],
        scratch_shapes=[
            pltpu.VMEM((CONV_HALO + tt, gw), F32),
            pltpu.VMEM((SHORT_HALO + tt, gw), F32),
            pltpu.VMEM((F32_SUBLANES - 1, CONV_HALO + tt - F32_SUBLANES, gw), F32),
        ],
        compiler_params=pltpu.CompilerParams(
            dimension_semantics=("parallel", "arbitrary"), vmem_limit_bytes=VMEM_LIMIT),
        name="convmix",
    )(pf3, pf3, pf3, pf3, pf3, pf3, pf3, pf3, pf3, dww, dwb, lng, lnb, scw)


def _sb_kernel(q_ref, k_ref, v_ref, o_ref, acc_ref, vh_ref, *, tq, n_kt):
    i = pl.program_id(1)
    lane_head = lax.broadcasted_iota(jnp.int32, (1, GROUP_WIDTH), 1) // HEAD_DIM
    row = lax.broadcasted_iota(jnp.int32, (tq, tq), 0)
    col = lax.broadcasted_iota(jnp.int32, (tq, tq), 1)
    later = jnp.where(row > col, 1.0, 0.0).astype(BF16)
    causal = col < row
    in_head = [lane_head == h for h in range(N_HEADS)]

    @pl.when(i == 0)
    def _():
        def fill(jt, _):
            vj = v_ref[0, pl.ds(pl.multiple_of(jt * tq, tq), tq), :]
            for h in range(N_HEADS):
                vh_ref[jt, h * tq:(h + 1) * tq, :] = jnp.where(in_head[h], vj, jnp.zeros_like(vj))
            return 0
        lax.fori_loop(0, n_kt, fill, 0)

    acc_ref[...] = jnp.zeros_like(acc_ref)
    q = q_ref[0]
    qh = [jnp.where(m, q, jnp.zeros_like(q)) for m in in_head]

    def block(j, runs, diag):
        kj = k_ref[0, pl.ds(pl.multiple_of(j * tq, tq), tq), :]
        ps, new_runs = [], []
        for h in range(N_HEADS):
            z = _dot_nt(qh[h], kj)
            lg = jnp.log(1.0 + jnp.exp2(jnp.abs(z) * (-LOG2E)))
            sp = jnp.maximum(z, 0.0) + lg
            if diag:
                sp = jnp.where(causal, sp, 0.0)
            after = _dot(sp.astype(BF16), later)
            p = jnp.exp(z - sp - after - runs[h])
            if diag:
                p = jnp.where(causal, p, 0.0)
            ps.append(p.astype(BF16))
            new_runs.append(runs[h] + jnp.sum(sp, axis=1, keepdims=True))
        acc_ref[...] += _dot(jnp.concatenate(ps, axis=1), vh_ref[j])
        return tuple(new_runs)

    zero = jnp.zeros((tq, 1), F32)
    runs = block(i, (zero,) * N_HEADS, True)

    def pair(jj, c):
        j = i - 1 - 2 * jj
        return block(j - 1, block(j, c, False), False)

    runs = lax.fori_loop(0, i // 2, pair, runs)

    @pl.when(i % 2 == 1)
    def _():
        block(0, runs, False)

    o_ref[0] = acc_ref[...].astype(o_ref.dtype)


def _stickbreak(qkv3, tq):
    b, t, _ = qkv3.shape
    gw = GROUP_WIDTH
    n_kt = t // tq
    kern = functools.partial(_sb_kernel, tq=tq, n_kt=n_kt)
    return pl.pallas_call(
        kern,
        grid=(b, n_kt),
        in_specs=[
            pl.BlockSpec((1, tq, gw), lambda bi, qi: (bi, qi, 0)),
            pl.BlockSpec((1, t, gw), lambda bi, qi: (bi, 0, 1)),
            pl.BlockSpec((1, t, gw), lambda bi, qi: (bi, 0, 2)),
        ],
        out_specs=pl.BlockSpec((1, tq, gw), lambda bi, qi: (bi, qi, 0)),
        out_shape=jax.ShapeDtypeStruct((b, t, gw), BF16),
        scratch_shapes=[pltpu.VMEM((tq, gw), F32),
                        pltpu.VMEM((n_kt, N_HEADS * tq, gw), BF16)],
        compiler_params=pltpu.CompilerParams(
            dimension_semantics=("parallel", "arbitrary"), vmem_limit_bytes=VMEM_LIMIT),
        name="stickbreak",
    )(qkv3, qkv3, qkv3)


def _ret_kernel(q_ref, k_ref, v_ref, g_ref, cos_ref, sin_ref, dintra_ref, qdec_ref, kdec_ref,
                cdec_ref, ng_ref, o_ref, state_ref, *, tt):
    gw = GROUP_WIDTH
    c = RET_CHUNK

    @pl.when(pl.program_id(1) == 0)
    def _():
        state_ref[...] = jnp.zeros_like(state_ref)

    lane = lax.broadcasted_iota(jnp.int32, (1, gw), 1)
    lane_head = lane // HEAD_DIM
    first_half = (lane % HEAD_DIM) < (HEAD_DIM // 2)
    rr = lax.broadcasted_iota(jnp.int32, (gw, gw), 0) // HEAD_DIM
    cc = lax.broadcasted_iota(jnp.int32, (gw, gw), 1) // HEAD_DIM
    same_head = rr == cc
    head_mean = jnp.where(same_head, 1.0 / HEAD_DIM, 0.0).astype(BF16)

    def rot(x, cos, sin):
        swapped = jnp.where(first_half, pltpu.roll(x, gw - HEAD_DIM // 2, 1),
                            pltpu.roll(x, HEAD_DIM // 2, 1))
        return x * cos + swapped * sin

    def seg_mean(x):
        hi, lo = _split_bf16(x)
        return _dot(hi, head_mean) + _dot(lo, head_mean)

    for ci in range(tt // c):
        rows = slice(ci * c, (ci + 1) * c)
        cos = cos_ref[rows, :]
        sin = sin_ref[rows, :]
        qr = rot(q_ref[0, rows, :], cos, sin)
        kr = rot(k_ref[0, rows, :], cos, sin) * (HEAD_DIM ** -0.5)
        v = v_ref[0, rows, :]
        kb = kr.astype(BF16)
        vb = v.astype(BF16)
        state = state_ref[...]

        o = _dot(qr.astype(BF16), state.astype(BF16)) * qdec_ref[...]
        for h in range(N_HEADS):
            in_head = lane_head == h
            qh = jnp.where(in_head, qr, 0.0).astype(BF16)
            inner = _dot_nt(qh, kb) * dintra_ref[h]
            vh = jnp.where(in_head, vb, jnp.zeros_like(vb))
            o = o + _dot(inner.astype(BF16), vh)

        kd_t = (kr * kdec_ref[...]).T.astype(BF16)
        kv = _dot(kd_t, vb)
        state_ref[...] = state * cdec_ref[...] + jnp.where(same_head, kv, 0.0)

        mu = seg_mean(o)
        oc = o - mu
        var = seg_mean(oc * oc)
        y = oc * lax.rsqrt(var + NORM_EPS) * ng_ref[...]
        g = g_ref[0, rows, :]
        o_ref[0, rows, :] = (g * jax.nn.sigmoid(g) * y).astype(o_ref.dtype)


def _retention_tables(t):
    d = HEAD_DIM
    half = d // 2
    inv_freq = ROPE_BASE ** (-jnp.arange(0, d, 2, dtype=F32) / d)
    ang = jnp.arange(t).astype(F32)[:, None] * inv_freq[None, :]
    cos, sin = jnp.cos(ang), jnp.sin(ang)
    cos_h = jnp.concatenate([cos, cos], axis=-1)
    sin_h = jnp.concatenate([-sin, sin], axis=-1)
    cos_t = jnp.tile(cos_h, (1, N_HEADS))
    sin_t = jnp.tile(sin_h, (1, N_HEADS))

    log_gamma = jnp.log1p(-jnp.exp2(-5.0 - jnp.arange(N_HEADS, dtype=F32)))
    idx = jnp.arange(RET_CHUNK, dtype=F32)
    diff = idx[:, None] - idx[None, :]
    causal = diff >= 0
    d_intra = jnp.where(causal[None], jnp.exp(jnp.where(causal, diff, 0.0)[None] * log_gamma[:, None, None]), 0.0)
    q_decay = jnp.exp((idx[None, :] + 1.0) * log_gamma[:, None])
    k_decay = jnp.exp((RET_CHUNK - 1.0 - idx[None, :]) * log_gamma[:, None])
    chunk_decay = jnp.exp(RET_CHUNK * log_gamma)
    qdec = jnp.repeat(q_decay.T, d, axis=1)
    kdec = jnp.repeat(k_decay.T, d, axis=1)
    cdec = jnp.repeat(chunk_decay, d)[None, :]
    return cos_t, sin_t, d_intra, qdec, kdec, cdec


def _retention(pf3, tables, norm_g, tt):
    b, t, _ = pf3.shape
    gw = GROUP_WIDTH
    cos_t, sin_t, d_intra, qdec, kdec, cdec = tables

    def main(col):
        return pl.BlockSpec((1, tt, gw), lambda bi, ti: (bi, ti, col))

    kern = functools.partial(_ret_kernel, tt=tt)
    return pl.pallas_call(
        kern,
        grid=(b, t // tt),
        in_specs=[
            main(PF_RET_Q), main(PF_RET_K), main(PF_RET_V), main(PF_RET_G),
            pl.BlockSpec((tt, gw), lambda bi, ti: (ti, 0)),
            pl.BlockSpec((tt, gw), lambda bi, ti: (ti, 0)),
            _resident((N_HEADS, RET_CHUNK, RET_CHUNK)),
            _resident((RET_CHUNK, gw)), _resident((RET_CHUNK, gw)),
            _resident((1, gw)), _resident((1, gw)),
        ],
        out_specs=pl.BlockSpec((1, tt, gw), lambda bi, ti: (bi, ti, 0)),
        out_shape=jax.ShapeDtypeStruct((b, t, gw), BF16),
        scratch_shapes=[pltpu.VMEM((gw, gw), F32)],
        compiler_params=pltpu.CompilerParams(
            dimension_semantics=("parallel", "arbitrary"), vmem_limit_bytes=VMEM_LIMIT),
        name="retention",
    )(pf3, pf3, pf3, pf3, cos_t, sin_t, d_intra, qdec, kdec, cdec, norm_g)


def _outproj_kernel(yc_ref, ysb_ref, yr_ref, ysc_ref, w_ref, g_ref, x_ref, o_ref):
    gw = GROUP_WIDTH
    mix = _dot(yc_ref[...], w_ref[0:gw, :])
    mix = mix + _dot(ysb_ref[...], w_ref[gw:2 * gw, :])
    mix = mix + _dot(yr_ref[...], w_ref[2 * gw:3 * gw, :])
    mix = mix + _dot(ysc_ref[...], w_ref[3 * gw:4 * gw, :])
    ms = jnp.mean(mix * mix, axis=-1, keepdims=True)
    o_ref[...] = x_ref[...] + mix * lax.rsqrt(ms + NORM_EPS) * g_ref[...]


def _outproj(ys, w, g, x2, tm):
    n, d = x2.shape
    gw = GROUP_WIDTH
    yspec = pl.BlockSpec((tm, gw), lambda i: (i, 0))
    return pl.pallas_call(
        _outproj_kernel,
        grid=(n // tm,),
        in_specs=[yspec, yspec, yspec, yspec, _resident(w.shape), _resident((1, d)),
                  pl.BlockSpec((tm, d), lambda i: (i, 0))],
        out_specs=pl.BlockSpec((tm, d), lambda i: (i, 0)),
        out_shape=jax.ShapeDtypeStruct((n, d), F32),
        compiler_params=pltpu.CompilerParams(
            dimension_semantics=("parallel",), vmem_limit_bytes=VMEM_LIMIT),
        name="outproj",
    )(*ys, w, g, x2)


def _ffn_kernel(x_ref, xh_ref, gpre_ref, gpost_ref, wup_ref, wconv_ref, wdown_ref, o_ref,
                xn_ref, hga_ref, hua_ref, hgb_ref, hub_ref, f_ref, *, tm, tiles_per_seq, n_chunks):
    i = pl.program_id(0)
    seq_start = (i % tiles_per_seq) == 0
    gpre = gpre_ref[...]

    def norm(x):
        ms = jnp.mean(x * x, axis=-1, keepdims=True)
        return x * lax.rsqrt(ms + NORM_EPS) * gpre

    x = x_ref[...]
    xn_ref[0:FFN_HALO, :] = jnp.where(seq_start, 0.0, norm(xh_ref[...])).astype(BF16)
    xn_ref[FFN_HALO:FFN_HALO + tm, :] = norm(x).astype(BF16)
    f_ref[...] = jnp.zeros_like(f_ref)

    def conv(h_ref, w):
        out = None
        for k in range(FFN_KERNEL):
            off = FFN_HALO - (FFN_KERNEL - 1) + k
            term = w[k:k + 1, :] * h_ref[off:off + tm, :]
            out = term if out is None else out + term
        return out

    def up_proj(c, h_refs):
        xn = xn_ref[...]
        h_refs[0][...] = _dot(xn, wup_ref[c])
        h_refs[1][...] = _dot(xn, wup_ref[n_chunks + c])

    def down_proj(c, h_refs):
        gate = conv(h_refs[0], wconv_ref[c])
        up = conv(h_refs[1], wconv_ref[n_chunks + c])
        act = (gate * jax.nn.sigmoid(gate) * up).astype(BF16)
        f_ref[...] += _dot(act, wdown_ref[c])

    slot_a = (hga_ref, hua_ref)
    slot_b = (hgb_ref, hub_ref)
    assert n_chunks % 2 == 1
    up_proj(0, slot_a)

    def chunk_pair(cp, _):
        c = 2 * cp
        up_proj(c + 1, slot_b)
        down_proj(c, slot_a)
        up_proj(c + 2, slot_a)
        down_proj(c + 1, slot_b)
        return 0

    lax.fori_loop(0, n_chunks // 2, chunk_pair, 0)
    down_proj(n_chunks - 1, slot_a)
    f = f_ref[...]
    ms = jnp.mean(f * f, axis=-1, keepdims=True)
    o_ref[...] = x + f * lax.rsqrt(ms + NORM_EPS) * gpost_ref[...]


def _ffn(x2, gpre, gpost, wup, wconv, wdown, tm, seq):
    n, d = x2.shape
    n_chunks = wdown.shape[0]
    fc = wdown.shape[1]
    per = tm // FFN_HALO
    kern = functools.partial(_ffn_kernel, tm=tm, tiles_per_seq=seq // tm, n_chunks=n_chunks)
    return pl.pallas_call(
        kern,
        grid=(n // tm,),
        in_specs=[
            pl.BlockSpec((tm, d), lambda i: (i, 0)),
            pl.BlockSpec((FFN_HALO, d), lambda i: (jnp.maximum(i * per - 1, 0), 0)),
            _resident((1, d)), _resident((1, d)),
            _resident(wup.shape), _resident(wconv.shape), _resident(wdown.shape),
        ],
        out_specs=pl.BlockSpec((tm, d), lambda i: (i, 0)),
        out_shape=jax.ShapeDtypeStruct((n, d), F32),
        scratch_shapes=[
            pltpu.VMEM((FFN_HALO + tm, d), BF16),
            pltpu.VMEM((FFN_HALO + tm, fc), F32),
            pltpu.VMEM((FFN_HALO + tm, fc), F32),
            pltpu.VMEM((FFN_HALO + tm, fc), F32),
            pltpu.VMEM((FFN_HALO + tm, fc), F32),
            pltpu.VMEM((tm, d), F32),
        ],
        compiler_params=pltpu.CompilerParams(
            dimension_semantics=("parallel",), vmem_limit_bytes=VMEM_LIMIT),
        name="convffn",
    )(x2, x2, gpre, gpost, wup, wconv, wdown)


def kernel(x, norm_mix_pre, norm_mix_post, norm_ffn_pre, norm_ffn_post, w_in, conf_dw_w, conf_dw_b, conf_ln_g, conf_ln_b, ret_norm_g, sc_conv_w, w_out, ffn_up, ffn_conv_w, ffn_down):
    b, t, d = x.shape
    depth = w_in.shape[0]
    d_ff = ffn_down.shape[1]
    n = b * t
    gw = GROUP_WIDTH
    assert w_in.shape[2] == 12 * gw and w_out.shape[1] == 4 * gw
    assert d_ff % FFN_CHUNK == 0
    n_chunks = d_ff // FFN_CHUNK

    tm = min(512, t)
    tt = min(512, t)
    tq = min(256, t)
    assert t % tm == 0 and t % tq == 0 and t % RET_CHUNK == 0

    tables = _retention_tables(t)
    x2 = x.reshape(n, d)
    for l in range(depth):
        row = lambda a: a[l][None, :]
        w_in_l = w_in[l].astype(BF16)
        w_out_l = w_out[l].astype(BF16)
        wup_l = ffn_up[l].astype(BF16).reshape(d, 2 * n_chunks, FFN_CHUNK).transpose(1, 0, 2)
        wconv_l = ffn_conv_w[l].reshape(FFN_KERNEL, 2 * n_chunks, FFN_CHUNK).transpose(1, 0, 2)
        wdown_l = ffn_down[l].astype(BF16).reshape(n_chunks, FFN_CHUNK, d)

        pf, qkv = _inproj(x2, row(norm_mix_pre), w_in_l, tm)
        pf3 = pf.reshape(b, t, N_PF * gw)
        qkv3 = qkv.reshape(b, t, 3 * gw)
        y_conf, y_sc = _convmix(pf3, conf_dw_w[l], row(conf_dw_b), row(conf_ln_g), row(conf_ln_b),
                                sc_conv_w[l], tt)
        y_sb = _stickbreak(qkv3, tq)
        y_ret = _retention(pf3, tables, row(ret_norm_g), tt)
        ys = [y.reshape(n, gw) for y in (y_conf, y_sb, y_ret, y_sc)]
        x2 = _outproj(ys, w_out_l, row(norm_mix_post), x2, tm)
        x2 = _ffn(x2, row(norm_ffn_pre), row(norm_ffn_post), wup_l, wconv_l, wdown_l, tm, t)
    return x2.reshape(b, t, d)
```

```python
import functools

import jax
import jax.numpy as jnp
from jax import lax
from jax.experimental import pallas as pl
from jax.experimental.pallas import tpu as pltpu

F32 = jnp.float32
BF16 = jnp.bfloat16

GROUP_WIDTH = 256
HEAD_DIM = 64
N_HEADS = GROUP_WIDTH // HEAD_DIM
CONF_KERNEL = 31
SHORT_KERNEL = 3
FFN_KERNEL = 3
RET_CHUNK = 128
ROPE_BASE = 10000.0
NORM_EPS = 1e-6
LOG2E = 1.4426950408889634

PF_CONF_A, PF_CONF_GATE, PF_RET_Q, PF_RET_K, PF_RET_V, PF_RET_G, PF_SC_B, PF_SC_C, PF_SC_H = range(9)
N_PF = 9

VMEM_LIMIT = 56 * 1024 * 1024
F32_SUBLANES = 8

CONV_HALO = 32
SHORT_HALO = 8
FFN_HALO = 16
FFN_CHUNK = 256


def _dot(a, b):
    return jnp.dot(a, b, preferred_element_type=F32)


def _dot_nt(a, b):
    return lax.dot_general(a, b, (((1,), (1,)), ((), ())), preferred_element_type=F32)


def _split_bf16(x):
    hi = x.astype(BF16)
    lo = (x - hi.astype(F32)).astype(BF16)
    return hi, lo


def _resident(shape):
    nd = len(shape)
    return pl.BlockSpec(shape, lambda *_: (0,) * nd, pipeline_mode=pl.Buffered(1))


def _inproj_kernel(x_ref, g_ref, w_ref, pf_ref, qkv_ref):
    x = x_ref[...]
    ms = jnp.mean(x * x, axis=-1, keepdims=True)
    h = (x * lax.rsqrt(ms + NORM_EPS) * g_ref[...]).astype(BF16)
    gw = GROUP_WIDTH
    pf_ref[:, 0:2 * gw] = _dot(h, w_ref[:, 0:2 * gw])
    qkv_ref[:, 0:gw] = (_dot(h, w_ref[:, 2 * gw:3 * gw]) * (HEAD_DIM ** -0.5)).astype(BF16)
    qkv_ref[:, gw:3 * gw] = _dot(h, w_ref[:, 3 * gw:5 * gw]).astype(BF16)
    pf_ref[:, 2 * gw:N_PF * gw] = _dot(h, w_ref[:, 5 * gw:12 * gw])


def _inproj(x2, g, w, tm):
    n, d = x2.shape
    d_in = w.shape[1]
    return pl.pallas_call(
        _inproj_kernel,
        grid=(n // tm,),
        in_specs=[
            pl.BlockSpec((tm, d), lambda i: (i, 0)),
            _resident((1, d)),
            _resident((d, d_in)),
        ],
        out_specs=[
            pl.BlockSpec((tm, N_PF * GROUP_WIDTH), lambda i: (i, 0)),
            pl.BlockSpec((tm, 3 * GROUP_WIDTH), lambda i: (i, 0)),
        ],
        out_shape=[
            jax.ShapeDtypeStruct((n, N_PF * GROUP_WIDTH), F32),
            jax.ShapeDtypeStruct((n, 3 * GROUP_WIDTH), BF16),
        ],
        compiler_params=pltpu.CompilerParams(
            dimension_semantics=("parallel",), vmem_limit_bytes=VMEM_LIMIT),
        name="inproj",
    )(x2, g, w)


def _convmix_kernel(a_ref, gate_ref, a_h_ref, gate_h_ref, scb_ref, scc_ref, sch_ref,
                    scc_h_ref, sch_h_ref, dww_ref, dwb_ref, lng_ref, lnb_ref, scw_ref,
                    yconf_ref, ysc_ref, hbuf, ubuf, hrot, *, tt, sub):
    t = pl.program_id(1)
    first = t == 0

    halo = a_h_ref[0] * jax.nn.sigmoid(gate_h_ref[0])
    hbuf[0:CONV_HALO, :] = jnp.where(first, 0.0, halo)
    hbuf[CONV_HALO:CONV_HALO + tt, :] = a_ref[0] * jax.nn.sigmoid(gate_ref[0])

    uhalo = scc_h_ref[0] * sch_h_ref[0]
    ubuf[0:SHORT_HALO, :] = jnp.where(first, 0.0, uhalo)
    ubuf[SHORT_HALO:SHORT_HALO + tt, :] = scc_ref[0] * sch_ref[0]

    n_rot = hrot.shape[1]
    for s in range(1, F32_SUBLANES):
        hrot[s - 1, :, :] = hbuf[s:s + n_rot, :]

    bias = dwb_ref[...]
    lng = lng_ref[...]
    lnb = lnb_ref[...]
    for r in range(tt // sub):
        base = r * sub
        acc = jnp.broadcast_to(bias, (sub, GROUP_WIDTH))
        for k in range(CONF_KERNEL):
            off = CONV_HALO - (CONF_KERNEL - 1) + k
            phase = off % F32_SUBLANES
            lo = off - phase + base
            if phase == 0:
                taps = hbuf[lo:lo + sub, :]
            else:
                taps = hrot[phase - 1, lo:lo + sub, :]
            acc = acc + dww_ref[k:k + 1, :] * taps
        mu = jnp.mean(acc, axis=-1, keepdims=True)
        xc = acc - mu
        var = jnp.mean(xc * xc, axis=-1, keepdims=True)
        y = xc * lax.rsqrt(var + NORM_EPS) * lng + lnb
        yconf_ref[0, base:base + sub, :] = (y * jax.nn.sigmoid(y)).astype(yconf_ref.dtype)

        conv = jnp.zeros((sub, GROUP_WIDTH), F32)
        for k in range(SHORT_KERNEL):
            off = SHORT_HALO - (SHORT_KERNEL - 1) + k + base
            conv = conv + scw_ref[k:k + 1, :] * ubuf[off:off + sub, :]
        ysc_ref[0, base:base + sub, :] = (scb_ref[0, base:base + sub, :] * conv).astype(ysc_ref.dtype)


def _convmix(pf3, dww, dwb, lng, lnb, scw, tt):
    b, t, _ = pf3.shape
    gw = GROUP_WIDTH
    sub = 64

    def main(col):
        return pl.BlockSpec((1, tt, gw), lambda bi, ti: (bi, ti, col))

    def halo(col, rows):
        per = tt // rows
        return pl.BlockSpec((1, rows, gw), lambda bi, ti: (bi, jnp.maximum(ti * per - 1, 0), col))

    kern = functools.partial(_convmix_kernel, tt=tt, sub=sub)
    return pl.pallas_call(
        kern,
        grid=(b, t // tt),
        in_specs=[
            main(PF_CONF_A), main(PF_CONF_GATE),
            halo(PF_CONF_A, CONV_HALO), halo(PF_CONF_GATE, CONV_HALO),
            main(PF_SC_B), main(PF_SC_C), main(PF_SC_H),
            halo(PF_SC_C, SHORT_HALO), halo(PF_SC_H, SHORT_HALO),
            _resident((CONF_KERNEL, gw)), _resident((1, gw)), _resident((1, gw)),
            _resident((1, gw)), _resident((SHORT_KERNEL, gw)),
        ],
        out_specs=[
            pl.BlockSpec((1, tt, gw), lambda bi, ti: (bi, ti, 0)),
            pl.BlockSpec((1, tt, gw), lambda bi, ti: (bi, ti, 0)),
        ],
        out_shape=[
            jax.ShapeDtypeStruct((b, t, gw), BF16),
            jax.ShapeDtypeStruct((b, t, gw), BF16),
        ],
        scratch_shapes=[
            pltpu.VMEM((CONV_HALO + tt, gw), F32),
            pltpu.VMEM((SHORT_HALO + tt, gw), F32),
            pltpu.VMEM((F32_SUBLANES - 1, CONV_HALO + tt - F32_SUBLANES, gw), F32),
        ],
        compiler_params=pltpu.CompilerParams(
            dimension_semantics=("parallel", "arbitrary"), vmem_limit_bytes=VMEM_LIMIT),
        name="convmix",
    )(pf3, pf3, pf3, pf3, pf3, pf3, pf3, pf3, pf3, dww, dwb, lng, lnb, scw)


def _sb_kernel(q_ref, k_ref, v_ref, o_ref, acc_ref, vh_ref, *, tk, n_kt):
    qi = pl.program_id(1)
    lane_head = lax.broadcasted_iota(jnp.int32, (1, GROUP_WIDTH), 1) // HEAD_DIM
    row = lax.broadcasted_iota(jnp.int32, (tk, tk), 0)
    col = lax.broadcasted_iota(jnp.int32, (tk, tk), 1)
    later = jnp.where(row > col, 1.0, 0.0).astype(BF16)
    causal = col < row
    in_head = [lane_head == h for h in range(N_HEADS)]

    @pl.when(qi == 0)
    def _():
        def fill(jt, _):
            vj = v_ref[0, pl.ds(pl.multiple_of(jt * tk, tk), tk), :]
            for h in range(N_HEADS):
                vh_ref[jt, h * tk:(h + 1) * tk, :] = jnp.where(in_head[h], vj, jnp.zeros_like(vj))
            return 0
        lax.fori_loop(0, n_kt, fill, 0)

    acc_ref[...] = jnp.zeros_like(acc_ref)
    q_sub = [q_ref[0, 0:tk, :], q_ref[0, tk:2 * tk, :]]
    qh = [[jnp.where(m, q, jnp.zeros_like(q)) for m in in_head] for q in q_sub]

    def block(j, runs, modes):
        kj = k_ref[0, pl.ds(pl.multiple_of(j * tk, tk), tk), :]
        new_runs = list(runs)
        for s in range(2):
            if modes[s] is None:
                continue
            diag = modes[s] == "diag"
            ps = []
            for h in range(N_HEADS):
                z = _dot_nt(qh[s][h], kj)
                lg = jnp.log(1.0 + jnp.exp2(jnp.abs(z) * (-LOG2E)))
                sp = jnp.maximum(z, 0.0) + lg
                if diag:
                    sp = jnp.where(causal, sp, 0.0)
                after = _dot(sp.astype(BF16), later)
                run = runs[s * N_HEADS + h]
                p = jnp.exp(z - sp - after - run)
                if diag:
                    p = jnp.where(causal, p, 0.0)
                ps.append(p.astype(BF16))
                new_runs[s * N_HEADS + h] = run + jnp.sum(sp, axis=1, keepdims=True)
            acc_ref[s * tk:(s + 1) * tk, :] += _dot(jnp.concatenate(ps, axis=1), vh_ref[j])
        return tuple(new_runs)

    zero = jnp.zeros((tk, 1), F32)
    runs = block(2 * qi + 1, (zero,) * (2 * N_HEADS), (None, "diag"))
    runs = block(2 * qi, runs, ("diag", "full"))

    def pair(jj, c):
        j = 2 * qi - 1 - 2 * jj
        return block(j - 1, block(j, c, ("full", "full")), ("full", "full"))

    lax.fori_loop(0, qi, pair, runs)
    o_ref[0] = acc_ref[...].astype(o_ref.dtype)


def _stickbreak(qkv3, tk):
    b, t, _ = qkv3.shape
    gw = GROUP_WIDTH
    n_kt = t // tk
    tq = 2 * tk
    kern = functools.partial(_sb_kernel, tk=tk, n_kt=n_kt)
    return pl.pallas_call(
        kern,
        grid=(b, t // tq),
        in_specs=[
            pl.BlockSpec((1, tq, gw), lambda bi, qi: (bi, qi, 0)),
            pl.BlockSpec((1, t, gw), lambda bi, qi: (bi, 0, 1)),
            pl.BlockSpec((1, t, gw), lambda bi, qi: (bi, 0, 2)),
        ],
        out_specs=pl.BlockSpec((1, tq, gw), lambda bi, qi: (bi, qi, 0)),
        out_shape=jax.ShapeDtypeStruct((b, t, gw), BF16),
        scratch_shapes=[pltpu.VMEM((tq, gw), F32),
                        pltpu.VMEM((n_kt, N_HEADS * tk, gw), BF16)],
        compiler_params=pltpu.CompilerParams(
            dimension_semantics=("parallel", "arbitrary"), vmem_limit_bytes=VMEM_LIMIT),
        name="stickbreak",
    )(qkv3, qkv3, qkv3)


def _ret_kernel(q_ref, k_ref, v_ref, g_ref, cos_ref, sin_ref, dintra_ref, qdec_ref, kdec_ref,
                cdec_ref, ng_ref, o_ref, state_ref, *, tt):
    gw = GROUP_WIDTH
    c = RET_CHUNK

    @pl.when(pl.program_id(1) == 0)
    def _():
        state_ref[...] = jnp.zeros_like(state_ref)

    lane = lax.broadcasted_iota(jnp.int32, (1, gw), 1)
    lane_head = lane // HEAD_DIM
    first_half = (lane % HEAD_DIM) < (HEAD_DIM // 2)
    rr = lax.broadcasted_iota(jnp.int32, (gw, gw), 0) // HEAD_DIM
    cc = lax.broadcasted_iota(jnp.int32, (gw, gw), 1) // HEAD_DIM
    same_head = rr == cc
    head_mean = jnp.where(same_head, 1.0 / HEAD_DIM, 0.0).astype(BF16)

    def rot(x, cos, sin):
        swapped = jnp.where(first_half, pltpu.roll(x, gw - HEAD_DIM // 2, 1),
                            pltpu.roll(x, HEAD_DIM // 2, 1))
        return x * cos + swapped * sin

    def seg_mean(x):
        hi, lo = _split_bf16(x)
        return _dot(hi, head_mean) + _dot(lo, head_mean)

    for ci in range(tt // c):
        rows = slice(ci * c, (ci + 1) * c)
        cos = cos_ref[rows, :]
        sin = sin_ref[rows, :]
        qr = rot(q_ref[0, rows, :], cos, sin)
        kr = rot(k_ref[0, rows, :], cos, sin) * (HEAD_DIM ** -0.5)
        v = v_ref[0, rows, :]
        kb = kr.astype(BF16)
        vb = v.astype(BF16)
        state = state_ref[...]

        o = _dot(qr.astype(BF16), state.astype(BF16)) * qdec_ref[...]
        for h in range(N_HEADS):
            in_head = lane_head == h
            qh = jnp.where(in_head, qr, 0.0).astype(BF16)
            inner = _dot_nt(qh, kb) * dintra_ref[h]
            vh = jnp.where(in_head, vb, jnp.zeros_like(vb))
            o = o + _dot(inner.astype(BF16), vh)

        kd_t = (kr * kdec_ref[...]).T.astype(BF16)
        kv = _dot(kd_t, vb)
        state_ref[...] = state * cdec_ref[...] + jnp.where(same_head, kv, 0.0)

        mu = seg_mean(o)
        oc = o - mu
        var = seg_mean(oc * oc)
        y = oc * lax.rsqrt(var + NORM_EPS) * ng_ref[...]
        g = g_ref[0, rows, :]
        o_ref[0, rows, :] = (g * jax.nn.sigmoid(g) * y).astype(o_ref.dtype)


def _retention_tables(t):
    d = HEAD_DIM
    half = d // 2
    inv_freq = ROPE_BASE ** (-jnp.arange(0, d, 2, dtype=F32) / d)
    ang = jnp.arange(t).astype(F32)[:, None] * inv_freq[None, :]
    cos, sin = jnp.cos(ang), jnp.sin(ang)
    cos_h = jnp.concatenate([cos, cos], axis=-1)
    sin_h = jnp.concatenate([-sin, sin], axis=-1)
    cos_t = jnp.tile(cos_h, (1, N_HEADS))
    sin_t = jnp.tile(sin_h, (1, N_HEADS))

    log_gamma = jnp.log1p(-jnp.exp2(-5.0 - jnp.arange(N_HEADS, dtype=F32)))
    idx = jnp.arange(RET_CHUNK, dtype=F32)
    diff = idx[:, None] - idx[None, :]
    causal = diff >= 0
    d_intra = jnp.where(causal[None], jnp.exp(jnp.where(causal, diff, 0.0)[None] * log_gamma[:, None, None]), 0.0)
    q_decay = jnp.exp((idx[None, :] + 1.0) * log_gamma[:, None])
    k_decay = jnp.exp((RET_CHUNK - 1.0 - idx[None, :]) * log_gamma[:, None])
    chunk_decay = jnp.exp(RET_CHUNK * log_gamma)
    qdec = jnp.repeat(q_decay.T, d, axis=1)
    kdec = jnp.repeat(k_decay.T, d, axis=1)
    cdec = jnp.repeat(chunk_decay, d)[None, :]
    return cos_t, sin_t, d_intra, qdec, kdec, cdec


def _retention(pf3, tables, norm_g, tt):
    b, t, _ = pf3.shape
    gw = GROUP_WIDTH
    cos_t, sin_t, d_intra, qdec, kdec, cdec = tables

    def main(col):
        return pl.BlockSpec((1, tt, gw), lambda bi, ti: (bi, ti, col))

    kern = functools.partial(_ret_kernel, tt=tt)
    return pl.pallas_call(
        kern,
        grid=(b, t // tt),
        in_specs=[
            main(PF_RET_Q), main(PF_RET_K), main(PF_RET_V), main(PF_RET_G),
            pl.BlockSpec((tt, gw), lambda bi, ti: (ti, 0)),
            pl.BlockSpec((tt, gw), lambda bi, ti: (ti, 0)),
            _resident((N_HEADS, RET_CHUNK, RET_CHUNK)),
            _resident((RET_CHUNK, gw)), _resident((RET_CHUNK, gw)),
            _resident((1, gw)), _resident((1, gw)),
        ],
        out_specs=pl.BlockSpec((1, tt, gw), lambda bi, ti: (bi, ti, 0)),
        out_shape=jax.ShapeDtypeStruct((b, t, gw), BF16),
        scratch_shapes=[pltpu.VMEM((gw, gw), F32)],
        compiler_params=pltpu.CompilerParams(
            dimension_semantics=("parallel", "arbitrary"), vmem_limit_bytes=VMEM_LIMIT),
        name="retention",
    )(pf3, pf3, pf3, pf3, cos_t, sin_t, d_intra, qdec, kdec, cdec, norm_g)


def _outproj_kernel(yc_ref, ysb_ref, yr_ref, ysc_ref, w_ref, g_ref, x_ref, o_ref):
    gw = GROUP_WIDTH
    mix = _dot(yc_ref[...], w_ref[0:gw, :])
    mix = mix + _dot(ysb_ref[...], w_ref[gw:2 * gw, :])
    mix = mix + _dot(yr_ref[...], w_ref[2 * gw:3 * gw, :])
    mix = mix + _dot(ysc_ref[...], w_ref[3 * gw:4 * gw, :])
    ms = jnp.mean(mix * mix, axis=-1, keepdims=True)
    o_ref[...] = x_ref[...] + mix * lax.rsqrt(ms + NORM_EPS) * g_ref[...]


def _outproj(ys, w, g, x2, tm):
    n, d = x2.shape
    gw = GROUP_WIDTH
    yspec = pl.BlockSpec((tm, gw), lambda i: (i, 0))
    return pl.pallas_call(
        _outproj_kernel,
        grid=(n // tm,),
        in_specs=[yspec, yspec, yspec, yspec, _resident(w.shape), _resident((1, d)),
                  pl.BlockSpec((tm, d), lambda i: (i, 0))],
        out_specs=pl.BlockSpec((tm, d), lambda i: (i, 0)),
        out_shape=jax.ShapeDtypeStruct((n, d), F32),
        compiler_params=pltpu.CompilerParams(
            dimension_semantics=("parallel",), vmem_limit_bytes=VMEM_LIMIT),
        name="outproj",
    )(*ys, w, g, x2)


def _ffn_kernel(x_ref, xh_ref, gpre_ref, gpost_ref, wup_ref, wconv_ref, wdown_ref, o_ref,
                xn_ref, hga_ref, hua_ref, hgb_ref, hub_ref, f_ref, *, tm, tiles_per_seq, n_chunks):
    i = pl.program_id(0)
    seq_start = (i % tiles_per_seq) == 0
    gpre = gpre_ref[...]

    def norm(x):
        ms = jnp.mean(x * x, axis=-1, keepdims=True)
        return x * lax.rsqrt(ms + NORM_EPS) * gpre

    x = x_ref[...]
    xn_ref[0:FFN_HALO, :] = jnp.where(seq_start, 0.0, norm(xh_ref[...])).astype(BF16)
    xn_ref[FFN_HALO:FFN_HALO + tm, :] = norm(x).astype(BF16)
    f_ref[...] = jnp.zeros_like(f_ref)

    def conv(h_ref, w):
        out = None
        for k in range(FFN_KERNEL):
            off = FFN_HALO - (FFN_KERNEL - 1) + k
            term = w[k:k + 1, :] * h_ref[off:off + tm, :]
            out = term if out is None else out + term
        return out

    fc = wdown_ref.shape[1]

    def cols(c):
        return pl.ds(pl.multiple_of(c * fc, fc), fc)

    def up_proj(c, h_refs):
        xn = xn_ref[...]
        h_refs[0][...] = _dot(xn, wup_ref[:, cols(c)])
        h_refs[1][...] = _dot(xn, wup_ref[:, cols(n_chunks + c)])

    def down_proj(c, h_refs):
        gate = conv(h_refs[0], wconv_ref[:, cols(c)])
        up = conv(h_refs[1], wconv_ref[:, cols(n_chunks + c)])
        act = (gate * jax.nn.sigmoid(gate) * up).astype(BF16)
        f_ref[...] += _dot(act, wdown_ref[c])

    slot_a = (hga_ref, hua_ref)
    slot_b = (hgb_ref, hub_ref)
    assert n_chunks % 2 == 1
    up_proj(0, slot_a)

    def chunk_pair(cp, _):
        c = 2 * cp
        up_proj(c + 1, slot_b)
        down_proj(c, slot_a)
        up_proj(c + 2, slot_a)
        down_proj(c + 1, slot_b)
        return 0

    lax.fori_loop(0, n_chunks // 2, chunk_pair, 0)
    down_proj(n_chunks - 1, slot_a)
    f = f_ref[...]
    ms = jnp.mean(f * f, axis=-1, keepdims=True)
    o_ref[...] = x + f * lax.rsqrt(ms + NORM_EPS) * gpost_ref[...]


def _ffn(x2, gpre, gpost, wup, wconv, wdown, tm, seq):
    n, d = x2.shape
    n_chunks = wdown.shape[0]
    fc = wdown.shape[1]
    per = tm // FFN_HALO
    kern = functools.partial(_ffn_kernel, tm=tm, tiles_per_seq=seq // tm, n_chunks=n_chunks)
    return pl.pallas_call(
        kern,
        grid=(n // tm,),
        in_specs=[
            pl.BlockSpec((tm, d), lambda i: (i, 0)),
            pl.BlockSpec((FFN_HALO, d), lambda i: (jnp.maximum(i * per - 1, 0), 0)),
            _resident((1, d)), _resident((1, d)),
            _resident(wup.shape), _resident(wconv.shape), _resident(wdown.shape),
        ],
        out_specs=pl.BlockSpec((tm, d), lambda i: (i, 0)),
        out_shape=jax.ShapeDtypeStruct((n, d), F32),
        scratch_shapes=[
            pltpu.VMEM((FFN_HALO + tm, d), BF16),
            pltpu.VMEM((FFN_HALO + tm, fc), F32),
            pltpu.VMEM((FFN_HALO + tm, fc), F32),
            pltpu.VMEM((FFN_HALO + tm, fc), F32),
            pltpu.VMEM((FFN_HALO + tm, fc), F32),
            pltpu.VMEM((tm, d), F32),
        ],
        compiler_params=pltpu.CompilerParams(
            dimension_semantics=("parallel",), vmem_limit_bytes=VMEM_LIMIT),
        name="convffn",
    )(x2, x2, gpre, gpost, wup, wconv, wdown)


def kernel(x, norm_mix_pre, norm_mix_post, norm_ffn_pre, norm_ffn_post, w_in, conf_dw_w, conf_dw_b, conf_ln_g, conf_ln_b, ret_norm_g, sc_conv_w, w_out, ffn_up, ffn_conv_w, ffn_down):
    b, t, d = x.shape
    depth = w_in.shape[0]
    d_ff = ffn_down.shape[1]
    n = b * t
    gw = GROUP_WIDTH
    assert w_in.shape[2] == 12 * gw and w_out.shape[1] == 4 * gw
    assert d_ff % FFN_CHUNK == 0
    n_chunks = d_ff // FFN_CHUNK

    tm = min(512, t)
    tm_out = min(1024, t)
    tt = min(512, t)
    tk = min(256, t // 2)
    assert t % tm_out == 0 and t % (2 * tk) == 0 and t % RET_CHUNK == 0

    tables = _retention_tables(t)
    x2 = x.reshape(n, d)
    for l in range(depth):
        row = lambda a: a[l][None, :]
        w_in_l = w_in[l].astype(BF16)
        w_out_l = w_out[l].astype(BF16)
        wup_l = ffn_up[l].astype(BF16)
        wconv_l = ffn_conv_w[l]
        wdown_l = ffn_down[l].astype(BF16).reshape(n_chunks, FFN_CHUNK, d)

        pf, qkv = _inproj(x2, row(norm_mix_pre), w_in_l, tm)
        pf3 = pf.reshape(b, t, N_PF * gw)
        qkv3 = qkv.reshape(b, t, 3 * gw)
        y_conf, y_sc = _convmix(pf3, conf_dw_w[l], row(conf_dw_b), row(conf_ln_g), row(conf_ln_b),
                                sc_conv_w[l], tt)
        y_sb = _stickbreak(qkv3, tk)
        y_ret = _retention(pf3, tables, row(ret_norm_g), tt)
        ys = [y.reshape(n, gw) for y in (y_conf, y_sb, y_ret, y_sc)]
        x2 = _outproj(ys, w_out_l, row(norm_mix_post), x2, tm_out)
        x2 = _ffn(x2, row(norm_ffn_pre), row(norm_ffn_post), wup_l, wconv_l, wdown_l, tm, t)
    return x2.reshape(b, t, d)
```

```python
import functools

import jax
import jax.numpy as jnp
from jax import lax
from jax.experimental import pallas as pl
from jax.experimental.pallas import tpu as pltpu

F32 = jnp.float32
BF16 = jnp.bfloat16

GROUP_WIDTH = 256
HEAD_DIM = 64
N_HEADS = GROUP_WIDTH // HEAD_DIM
CONF_KERNEL = 31
SHORT_KERNEL = 3
FFN_KERNEL = 3
RET_CHUNK = 128
ROPE_BASE = 10000.0
NORM_EPS = 1e-6
LOG2E = 1.4426950408889634

PF_CONF_A, PF_CONF_GATE, PF_RET_Q, PF_RET_K, PF_RET_V, PF_RET_G, PF_SC_B, PF_SC_C, PF_SC_H = range(9)
N_PF = 9

VMEM_LIMIT = 56 * 1024 * 1024
F32_SUBLANES = 8

CONV_HALO = 32
SHORT_HALO = 8
FFN_HALO = 16
FFN_CHUNK = 256


def _layer_resident(stack, l):
    shape = stack.shape[1:]
    return pl.BlockSpec((None,) + shape, lambda *_: (l,) + (0,) * len(shape),
                        pipeline_mode=pl.Buffered(1))


def _dot(a, b):
    return jnp.dot(a, b, preferred_element_type=F32)


def _dot_nt(a, b):
    return lax.dot_general(a, b, (((1,), (1,)), ((), ())), preferred_element_type=F32)


def _split_bf16(x):
    hi = x.astype(BF16)
    lo = (x - hi.astype(F32)).astype(BF16)
    return hi, lo


def _resident(shape):
    nd = len(shape)
    return pl.BlockSpec(shape, lambda *_: (0,) * nd, pipeline_mode=pl.Buffered(1))


def _inproj_kernel(x_ref, g_ref, w_ref, pf_ref, qkv_ref):
    x = x_ref[...]
    ms = jnp.mean(x * x, axis=-1, keepdims=True)
    h = (x * lax.rsqrt(ms + NORM_EPS) * g_ref[...]).astype(BF16)
    gw = GROUP_WIDTH
    pf_ref[:, 0:2 * gw] = _dot(h, w_ref[:, 0:2 * gw])
    qkv_ref[:, 0:gw] = (_dot(h, w_ref[:, 2 * gw:3 * gw]) * (HEAD_DIM ** -0.5)).astype(BF16)
    qkv_ref[:, gw:3 * gw] = _dot(h, w_ref[:, 3 * gw:5 * gw]).astype(BF16)
    pf_ref[:, 2 * gw:N_PF * gw] = _dot(h, w_ref[:, 5 * gw:12 * gw])


def _inproj(x2, g, w, l, tm):
    n, d = x2.shape
    d_in = w.shape[2]
    return pl.pallas_call(
        _inproj_kernel,
        grid=(n // tm,),
        in_specs=[
            pl.BlockSpec((tm, d), lambda i: (i, 0)),
            _resident((1, d)),
            _layer_resident(w, l),
        ],
        out_specs=[
            pl.BlockSpec((tm, N_PF * GROUP_WIDTH), lambda i: (i, 0)),
            pl.BlockSpec((tm, 3 * GROUP_WIDTH), lambda i: (i, 0)),
        ],
        out_shape=[
            jax.ShapeDtypeStruct((n, N_PF * GROUP_WIDTH), F32),
            jax.ShapeDtypeStruct((n, 3 * GROUP_WIDTH), BF16),
        ],
        compiler_params=pltpu.CompilerParams(
            dimension_semantics=("parallel",), vmem_limit_bytes=VMEM_LIMIT),
        name="inproj",
    )(x2, g, w)


def _convmix_kernel(a_ref, gate_ref, a_h_ref, gate_h_ref, scb_ref, scc_ref, sch_ref,
                    scc_h_ref, sch_h_ref, dww_ref, dwb_ref, lng_ref, lnb_ref, scw_ref,
                    yconf_ref, ysc_ref, hbuf, ubuf, hrot, *, tt, sub):
    t = pl.program_id(1)
    first = t == 0

    halo = a_h_ref[0] * jax.nn.sigmoid(gate_h_ref[0])
    hbuf[0:CONV_HALO, :] = jnp.where(first, 0.0, halo)
    hbuf[CONV_HALO:CONV_HALO + tt, :] = a_ref[0] * jax.nn.sigmoid(gate_ref[0])

    uhalo = scc_h_ref[0] * sch_h_ref[0]
    ubuf[0:SHORT_HALO, :] = jnp.where(first, 0.0, uhalo)
    ubuf[SHORT_HALO:SHORT_HALO + tt, :] = scc_ref[0] * sch_ref[0]

    n_rot = hrot.shape[1]
    for s in range(1, F32_SUBLANES):
        hrot[s - 1, :, :] = hbuf[s:s + n_rot, :]

    bias = dwb_ref[...]
    lng = lng_ref[...]
    lnb = lnb_ref[...]
    for r in range(tt // sub):
        base = r * sub
        acc = jnp.broadcast_to(bias, (sub, GROUP_WIDTH))
        for k in range(CONF_KERNEL):
            off = CONV_HALO - (CONF_KERNEL - 1) + k
            phase = off % F32_SUBLANES
            lo = off - phase + base
            if phase == 0:
                taps = hbuf[lo:lo + sub, :]
            else:
                taps = hrot[phase - 1, lo:lo + sub, :]
            acc = acc + dww_ref[k:k + 1, :] * taps
        mu = jnp.mean(acc, axis=-1, keepdims=True)
        xc = acc - mu
        var = jnp.mean(xc * xc, axis=-1, keepdims=True)
        y = xc * lax.rsqrt(var + NORM_EPS) * lng + lnb
        yconf_ref[0, base:base + sub, :] = (y * jax.nn.sigmoid(y)).astype(yconf_ref.dtype)

        conv = jnp.zeros((sub, GROUP_WIDTH), F32)
        for k in range(SHORT_KERNEL):
            off = SHORT_HALO - (SHORT_KERNEL - 1) + k + base
            conv = conv + scw_ref[k:k + 1, :] * ubuf[off:off + sub, :]
        ysc_ref[0, base:base + sub, :] = (scb_ref[0, base:base + sub, :] * conv).astype(ysc_ref.dtype)


def _convmix(pf3, dww, dwb, lng, lnb, scw, tt):
    b, t, _ = pf3.shape
    gw = GROUP_WIDTH
    sub = 128

    def main(col):
        return pl.BlockSpec((1, tt, gw), lambda bi, ti: (bi, ti, col))

    def halo(col, rows):
        per = tt // rows
        return pl.BlockSpec((1, rows, gw), lambda bi, ti: (bi, jnp.maximum(ti * per - 1, 0), col))

    kern = functools.partial(_convmix_kernel, tt=tt, sub=sub)
    return pl.pallas_call(
        kern,
        grid=(b, t // tt),
        in_specs=[
            main(PF_CONF_A), main(PF_CONF_GATE),
            halo(PF_CONF_A, CONV_HALO), halo(PF_CONF_GATE, CONV_HALO),
            main(PF_SC_B), main(PF_SC_C), main(PF_SC_H),
            halo(PF_SC_C, SHORT_HALO), halo(PF_SC_H, SHORT_HALO),
            _resident((CONF_KERNEL, gw)), _resident((1, gw)), _resident((1, gw)),
            _resident((1, gw)), _resident((SHORT_KERNEL, gw)),
        ],
        out_specs=[
            pl.BlockSpec((1, tt, gw), lambda bi, ti: (bi, ti, 0)),
            pl.BlockSpec((1, tt, gw), lambda bi, ti: (bi, ti, 0)),
        ],
        out_shape=[
            jax.ShapeDtypeStruct((b, t, gw), BF16),
            jax.ShapeDtypeStruct((b, t, gw), BF16),
        ],
        scratch_shapes=[
            pltpu.VMEM((CONV_HALO + tt, gw), F32),
            pltpu.VMEM((SHORT_HALO + tt, gw), F32),
            pltpu.VMEM((F32_SUBLANES - 1, CONV_HALO + tt - F32_SUBLANES, gw), F32),
        ],
        compiler_params=pltpu.CompilerParams(
            dimension_semantics=("parallel", "arbitrary"), vmem_limit_bytes=VMEM_LIMIT),
        name="convmix",
    )(pf3, pf3, pf3, pf3, pf3, pf3, pf3, pf3, pf3, dww, dwb, lng, lnb, scw)


def _sb_kernel(q_ref, k_ref, v_ref, o_ref, acc_ref, vh_ref, *, tk, n_kt):
    qi = pl.program_id(1)
    lane_head = lax.broadcasted_iota(jnp.int32, (1, GROUP_WIDTH), 1) // HEAD_DIM
    row = lax.broadcasted_iota(jnp.int32, (tk, tk), 0)
    col = lax.broadcasted_iota(jnp.int32, (tk, tk), 1)
    later = jnp.where(row > col, 1.0, 0.0).astype(BF16)
    causal = col < row
    in_head = [lane_head == h for h in range(N_HEADS)]

    @pl.when(qi == 0)
    def _():
        def fill(jt, _):
            vj = v_ref[0, pl.ds(pl.multiple_of(jt * tk, tk), tk), :]
            for h in range(N_HEADS):
                vh_ref[jt, h * tk:(h + 1) * tk, :] = jnp.where(in_head[h], vj, jnp.zeros_like(vj))
            return 0
        lax.fori_loop(0, n_kt, fill, 0)

    acc_ref[...] = jnp.zeros_like(acc_ref)
    q_sub = [q_ref[0, 0:tk, :], q_ref[0, tk:2 * tk, :]]
    qh = [[jnp.where(m, q, jnp.zeros_like(q)) for m in in_head] for q in q_sub]

    def block(j, runs, modes):
        kj = k_ref[0, pl.ds(pl.multiple_of(j * tk, tk), tk), :]
        new_runs = list(runs)
        for s in range(2):
            if modes[s] is None:
                continue
            diag = modes[s] == "diag"
            ps = []
            for h in range(N_HEADS):
                z = _dot_nt(qh[s][h], kj)
                lg = jnp.log(1.0 + jnp.exp2(jnp.abs(z) * (-LOG2E)))
                sp = jnp.maximum(z, 0.0) + lg
                if diag:
                    sp = jnp.where(causal, sp, 0.0)
                after = _dot(sp.astype(BF16), later)
                run = runs[s * N_HEADS + h]
                p = jnp.exp(z - sp - after - run)
                if diag:
                    p = jnp.where(causal, p, 0.0)
                ps.append(p.astype(BF16))
                new_runs[s * N_HEADS + h] = run + jnp.sum(sp, axis=1, keepdims=True)
            acc_ref[s * tk:(s + 1) * tk, :] += _dot(jnp.concatenate(ps, axis=1), vh_ref[j])
        return tuple(new_runs)

    zero = jnp.zeros((tk, 1), F32)
    runs = block(2 * qi + 1, (zero,) * (2 * N_HEADS), (None, "diag"))
    runs = block(2 * qi, runs, ("diag", "full"))

    def pair(jj, c):
        j = 2 * qi - 1 - 2 * jj
        return block(j - 1, block(j, c, ("full", "full")), ("full", "full"))

    lax.fori_loop(0, qi, pair, runs)
    o_ref[0] = acc_ref[...].astype(o_ref.dtype)


def _stickbreak(qkv3, tk):
    b, t, _ = qkv3.shape
    gw = GROUP_WIDTH
    n_kt = t // tk
    tq = 2 * tk
    kern = functools.partial(_sb_kernel, tk=tk, n_kt=n_kt)
    return pl.pallas_call(
        kern,
        grid=(b, t // tq),
        in_specs=[
            pl.BlockSpec((1, tq, gw), lambda bi, qi: (bi, qi, 0)),
            pl.BlockSpec((1, t, gw), lambda bi, qi: (bi, 0, 1)),
            pl.BlockSpec((1, t, gw), lambda bi, qi: (bi, 0, 2)),
        ],
        out_specs=pl.BlockSpec((1, tq, gw), lambda bi, qi: (bi, qi, 0)),
        out_shape=jax.ShapeDtypeStruct((b, t, gw), BF16),
        scratch_shapes=[pltpu.VMEM((tq, gw), F32),
                        pltpu.VMEM((n_kt, N_HEADS * tk, gw), BF16)],
        compiler_params=pltpu.CompilerParams(
            dimension_semantics=("parallel", "arbitrary"), vmem_limit_bytes=VMEM_LIMIT),
        name="stickbreak",
    )(qkv3, qkv3, qkv3)


def _ret_kernel(q_ref, k_ref, v_ref, g_ref, cos_ref, sin_ref, dintra_ref, qdec_ref, kdec_ref,
                cdec_ref, ng_ref, o_ref, state_ref, *, tt):
    gw = GROUP_WIDTH
    c = RET_CHUNK

    @pl.when(pl.program_id(1) == 0)
    def _():
        state_ref[...] = jnp.zeros_like(state_ref)

    lane = lax.broadcasted_iota(jnp.int32, (1, gw), 1)
    lane_head = lane // HEAD_DIM
    first_half = (lane % HEAD_DIM) < (HEAD_DIM // 2)
    rr = lax.broadcasted_iota(jnp.int32, (gw, gw), 0) // HEAD_DIM
    cc = lax.broadcasted_iota(jnp.int32, (gw, gw), 1) // HEAD_DIM
    same_head = rr == cc
    head_mean = jnp.where(same_head, 1.0 / HEAD_DIM, 0.0).astype(BF16)

    def rot(x, cos, sin):
        swapped = jnp.where(first_half, pltpu.roll(x, gw - HEAD_DIM // 2, 1),
                            pltpu.roll(x, HEAD_DIM // 2, 1))
        return x * cos + swapped * sin

    def seg_mean(x):
        hi, lo = _split_bf16(x)
        return _dot(hi, head_mean) + _dot(lo, head_mean)

    for ci in range(tt // c):
        rows = slice(ci * c, (ci + 1) * c)
        cos = cos_ref[rows, :]
        sin = sin_ref[rows, :]
        qr = rot(q_ref[0, rows, :], cos, sin)
        kr = rot(k_ref[0, rows, :], cos, sin) * (HEAD_DIM ** -0.5)
        v = v_ref[0, rows, :]
        kb = kr.astype(BF16)
        vb = v.astype(BF16)
        state = state_ref[...]

        o = _dot(qr.astype(BF16), state.astype(BF16)) * qdec_ref[...]
        for h in range(N_HEADS):
            in_head = lane_head == h
            qh = jnp.where(in_head, qr, 0.0).astype(BF16)
            inner = _dot_nt(qh, kb) * dintra_ref[h]
            vh = jnp.where(in_head, vb, jnp.zeros_like(vb))
            o = o + _dot(inner.astype(BF16), vh)

        kd_t = (kr * kdec_ref[...]).T.astype(BF16)
        kv = _dot(kd_t, vb)
        state_ref[...] = state * cdec_ref[...] + jnp.where(same_head, kv, 0.0)

        mu = seg_mean(o)
        oc = o - mu
        var = seg_mean(oc * oc)
        y = oc * lax.rsqrt(var + NORM_EPS) * ng_ref[...]
        g = g_ref[0, rows, :]
        o_ref[0, rows, :] = (g * jax.nn.sigmoid(g) * y).astype(o_ref.dtype)


def _retention_tables(t):
    d = HEAD_DIM
    half = d // 2
    inv_freq = ROPE_BASE ** (-jnp.arange(0, d, 2, dtype=F32) / d)
    ang = jnp.arange(t).astype(F32)[:, None] * inv_freq[None, :]
    cos, sin = jnp.cos(ang), jnp.sin(ang)
    cos_h = jnp.concatenate([cos, cos], axis=-1)
    sin_h = jnp.concatenate([-sin, sin], axis=-1)
    cos_t = jnp.tile(cos_h, (1, N_HEADS))
    sin_t = jnp.tile(sin_h, (1, N_HEADS))

    log_gamma = jnp.log1p(-jnp.exp2(-5.0 - jnp.arange(N_HEADS, dtype=F32)))
    idx = jnp.arange(RET_CHUNK, dtype=F32)
    diff = idx[:, None] - idx[None, :]
    causal = diff >= 0
    d_intra = jnp.where(causal[None], jnp.exp(jnp.where(causal, diff, 0.0)[None] * log_gamma[:, None, None]), 0.0)
    q_decay = jnp.exp((idx[None, :] + 1.0) * log_gamma[:, None])
    k_decay = jnp.exp((RET_CHUNK - 1.0 - idx[None, :]) * log_gamma[:, None])
    chunk_decay = jnp.exp(RET_CHUNK * log_gamma)
    qdec = jnp.repeat(q_decay.T, d, axis=1)
    kdec = jnp.repeat(k_decay.T, d, axis=1)
    cdec = jnp.repeat(chunk_decay, d)[None, :]
    return cos_t, sin_t, d_intra, qdec, kdec, cdec


def _retention(pf3, tables, norm_g, tt):
    b, t, _ = pf3.shape
    gw = GROUP_WIDTH
    cos_t, sin_t, d_intra, qdec, kdec, cdec = tables

    def main(col):
        return pl.BlockSpec((1, tt, gw), lambda bi, ti: (bi, ti, col))

    kern = functools.partial(_ret_kernel, tt=tt)
    return pl.pallas_call(
        kern,
        grid=(b, t // tt),
        in_specs=[
            main(PF_RET_Q), main(PF_RET_K), main(PF_RET_V), main(PF_RET_G),
            pl.BlockSpec((tt, gw), lambda bi, ti: (ti, 0)),
            pl.BlockSpec((tt, gw), lambda bi, ti: (ti, 0)),
            _resident((N_HEADS, RET_CHUNK, RET_CHUNK)),
            _resident((RET_CHUNK, gw)), _resident((RET_CHUNK, gw)),
            _resident((1, gw)), _resident((1, gw)),
        ],
        out_specs=pl.BlockSpec((1, tt, gw), lambda bi, ti: (bi, ti, 0)),
        out_shape=jax.ShapeDtypeStruct((b, t, gw), BF16),
        scratch_shapes=[pltpu.VMEM((gw, gw), F32)],
        compiler_params=pltpu.CompilerParams(
            dimension_semantics=("parallel", "arbitrary"), vmem_limit_bytes=VMEM_LIMIT),
        name="retention",
    )(pf3, pf3, pf3, pf3, cos_t, sin_t, d_intra, qdec, kdec, cdec, norm_g)


def _outproj_kernel(yc_ref, ysb_ref, yr_ref, ysc_ref, w_ref, g_ref, x_ref, o_ref):
    gw = GROUP_WIDTH
    mix = _dot(yc_ref[...], w_ref[0:gw, :])
    mix = mix + _dot(ysb_ref[...], w_ref[gw:2 * gw, :])
    mix = mix + _dot(yr_ref[...], w_ref[2 * gw:3 * gw, :])
    mix = mix + _dot(ysc_ref[...], w_ref[3 * gw:4 * gw, :])
    ms = jnp.mean(mix * mix, axis=-1, keepdims=True)
    o_ref[...] = x_ref[...] + mix * lax.rsqrt(ms + NORM_EPS) * g_ref[...]


def _outproj(ys, w, l, g, x2, tm):
    n, d = x2.shape
    gw = GROUP_WIDTH
    yspec = pl.BlockSpec((tm, gw), lambda i: (i, 0))
    return pl.pallas_call(
        _outproj_kernel,
        grid=(n // tm,),
        in_specs=[yspec, yspec, yspec, yspec, _layer_resident(w, l), _resident((1, d)),
                  pl.BlockSpec((tm, d), lambda i: (i, 0))],
        out_specs=pl.BlockSpec((tm, d), lambda i: (i, 0)),
        out_shape=jax.ShapeDtypeStruct((n, d), F32),
        compiler_params=pltpu.CompilerParams(
            dimension_semantics=("parallel",), vmem_limit_bytes=VMEM_LIMIT),
        name="outproj",
    )(*ys, w, g, x2)


def _ffn_kernel(x_ref, xh_ref, gpre_ref, gpost_ref, wup_ref, wconv_ref, wdown_ref, o_ref,
                xn_ref, hga_ref, hua_ref, hgb_ref, hub_ref, f_ref, *, tm, tiles_per_seq, n_chunks):
    i = pl.program_id(0)
    seq_start = (i % tiles_per_seq) == 0
    gpre = gpre_ref[...]

    def norm(x):
        ms = jnp.mean(x * x, axis=-1, keepdims=True)
        return x * lax.rsqrt(ms + NORM_EPS) * gpre

    x = x_ref[...]
    xn_ref[0:FFN_HALO, :] = jnp.where(seq_start, 0.0, norm(xh_ref[...])).astype(BF16)
    xn_ref[FFN_HALO:FFN_HALO + tm, :] = norm(x).astype(BF16)
    f_ref[...] = jnp.zeros_like(f_ref)

    def conv(h_ref, w):
        out = None
        for k in range(FFN_KERNEL):
            off = FFN_HALO - (FFN_KERNEL - 1) + k
            term = w[k:k + 1, :] * h_ref[off:off + tm, :]
            out = term if out is None else out + term
        return out

    fc = wdown_ref.shape[1]

    def cols(c):
        return pl.ds(pl.multiple_of(c * fc, fc), fc)

    def up_proj(c, h_refs):
        xn = xn_ref[...]
        h_refs[0][...] = _dot(xn, wup_ref[:, cols(c)])
        h_refs[1][...] = _dot(xn, wup_ref[:, cols(n_chunks + c)])

    def down_proj(c, h_refs):
        gate = conv(h_refs[0], wconv_ref[:, cols(c)])
        up = conv(h_refs[1], wconv_ref[:, cols(n_chunks + c)])
        act = (gate * jax.nn.sigmoid(gate) * up).astype(BF16)
        f_ref[...] += _dot(act, wdown_ref[c])

    slot_a = (hga_ref, hua_ref)
    slot_b = (hgb_ref, hub_ref)
    assert n_chunks % 2 == 1
    up_proj(0, slot_a)

    def chunk_pair(cp, _):
        c = 2 * cp
        up_proj(c + 1, slot_b)
        down_proj(c, slot_a)
        up_proj(c + 2, slot_a)
        down_proj(c + 1, slot_b)
        return 0

    lax.fori_loop(0, n_chunks // 2, chunk_pair, 0)
    down_proj(n_chunks - 1, slot_a)
    f = f_ref[...]
    ms = jnp.mean(f * f, axis=-1, keepdims=True)
    o_ref[...] = x + f * lax.rsqrt(ms + NORM_EPS) * gpost_ref[...]


def _ffn(x2, gpre, gpost, wup, wconv, wdown, l, tm, seq):
    n, d = x2.shape
    n_chunks = wdown.shape[1]
    fc = wdown.shape[2]
    per = tm // FFN_HALO
    kern = functools.partial(_ffn_kernel, tm=tm, tiles_per_seq=seq // tm, n_chunks=n_chunks)
    return pl.pallas_call(
        kern,
        grid=(n // tm,),
        in_specs=[
            pl.BlockSpec((tm, d), lambda i: (i, 0)),
            pl.BlockSpec((FFN_HALO, d), lambda i: (jnp.maximum(i * per - 1, 0), 0)),
            _resident((1, d)), _resident((1, d)),
            _layer_resident(wup, l), _layer_resident(wconv, l), _layer_resident(wdown, l),
        ],
        out_specs=pl.BlockSpec((tm, d), lambda i: (i, 0)),
        out_shape=jax.ShapeDtypeStruct((n, d), F32),
        scratch_shapes=[
            pltpu.VMEM((FFN_HALO + tm, d), BF16),
            pltpu.VMEM((FFN_HALO + tm, fc), F32),
            pltpu.VMEM((FFN_HALO + tm, fc), F32),
            pltpu.VMEM((FFN_HALO + tm, fc), F32),
            pltpu.VMEM((FFN_HALO + tm, fc), F32),
            pltpu.VMEM((tm, d), F32),
        ],
        compiler_params=pltpu.CompilerParams(
            dimension_semantics=("parallel",), vmem_limit_bytes=VMEM_LIMIT),
        name="convffn",
    )(x2, x2, gpre, gpost, wup, wconv, wdown)


def kernel(x, norm_mix_pre, norm_mix_post, norm_ffn_pre, norm_ffn_post, w_in, conf_dw_w, conf_dw_b, conf_ln_g, conf_ln_b, ret_norm_g, sc_conv_w, w_out, ffn_up, ffn_conv_w, ffn_down):
    b, t, d = x.shape
    depth = w_in.shape[0]
    d_ff = ffn_down.shape[1]
    n = b * t
    gw = GROUP_WIDTH
    assert w_in.shape[2] == 12 * gw and w_out.shape[1] == 4 * gw
    assert d_ff % FFN_CHUNK == 0
    n_chunks = d_ff // FFN_CHUNK

    tm_in = min(1024, t)
    tm_out = min(1024, t)
    tm_ffn = min(512, t)
    tt_conv = min(1024, t)
    tt_ret = min(2048, t)
    tk = min(256, t // 2)
    assert t % tm_in == 0 and t % tt_ret == 0 and t % (2 * tk) == 0 and t % RET_CHUNK == 0

    tables = _retention_tables(t)
    w_in_b = w_in.astype(BF16)
    w_out_b = w_out.astype(BF16)
    wup_b = ffn_up.astype(BF16)
    wdown_b = ffn_down.astype(BF16).reshape(depth, n_chunks, FFN_CHUNK, d)
    x2 = x.reshape(n, d)
    for l in range(depth):
        row = lambda a: a[l][None, :]
        pf, qkv = _inproj(x2, row(norm_mix_pre), w_in_b, l, tm_in)
        pf3 = pf.reshape(b, t, N_PF * gw)
        qkv3 = qkv.reshape(b, t, 3 * gw)
        y_conf, y_sc = _convmix(pf3, conf_dw_w[l], row(conf_dw_b), row(conf_ln_g), row(conf_ln_b),
                                sc_conv_w[l], tt_conv)
        y_sb = _stickbreak(qkv3, tk)
        y_ret = _retention(pf3, tables, row(ret_norm_g), tt_ret)
        ys = [y.reshape(n, gw) for y in (y_conf, y_sb, y_ret, y_sc)]
        x2 = _outproj(ys, w_out_b, l, row(norm_mix_post), x2, tm_out)
        x2 = _ffn(x2, row(norm_ffn_pre), row(norm_ffn_post), wup_b, ffn_conv_w, wdown_b, l, tm_ffn, t)
    return x2.reshape(b, t, d)
```

```python
import functools

import jax
import jax.numpy as jnp
from jax import lax
from jax.experimental import pallas as pl
from jax.experimental.pallas import tpu as pltpu

F32 = jnp.float32
BF16 = jnp.bfloat16

GROUP_WIDTH = 256
HEAD_DIM = 64
N_HEADS = GROUP_WIDTH // HEAD_DIM
CONF_KERNEL = 31
SHORT_KERNEL = 3
FFN_KERNEL = 3
RET_CHUNK = 128
ROPE_BASE = 10000.0
NORM_EPS = 1e-6
LOG2E = 1.4426950408889634

PF_CONF_A, PF_CONF_GATE, PF_RET_Q, PF_RET_K, PF_RET_V, PF_RET_G, PF_SC_B, PF_SC_C, PF_SC_H = range(9)
N_PF = 9

VMEM_LIMIT = 56 * 1024 * 1024
F32_SUBLANES = 8

CONV_HALO = 32
SHORT_HALO = 8
FFN_HALO = 16
FFN_CHUNK = 256


def _layer_resident(stack, l):
    shape = stack.shape[1:]
    return pl.BlockSpec((None,) + shape, lambda *_: (l,) + (0,) * len(shape),
                        pipeline_mode=pl.Buffered(1))


def _dot(a, b):
    return jnp.dot(a, b, preferred_element_type=F32)


def _dot_nt(a, b):
    return lax.dot_general(a, b, (((1,), (1,)), ((), ())), preferred_element_type=F32)


def _split_bf16(x):
    hi = x.astype(BF16)
    lo = (x - hi.astype(F32)).astype(BF16)
    return hi, lo


def _resident(shape):
    nd = len(shape)
    return pl.BlockSpec(shape, lambda *_: (0,) * nd, pipeline_mode=pl.Buffered(1))


def _inproj_kernel(x_ref, g_ref, w_ref, pf_ref, qkv_ref):
    x = x_ref[...]
    ms = jnp.mean(x * x, axis=-1, keepdims=True)
    h = (x * lax.rsqrt(ms + NORM_EPS) * g_ref[...]).astype(BF16)
    gw = GROUP_WIDTH
    pf_ref[:, 0:2 * gw] = _dot(h, w_ref[:, 0:2 * gw])
    qkv_ref[:, 0:gw] = (_dot(h, w_ref[:, 2 * gw:3 * gw]) * (HEAD_DIM ** -0.5)).astype(BF16)
    qkv_ref[:, gw:3 * gw] = _dot(h, w_ref[:, 3 * gw:5 * gw]).astype(BF16)
    pf_ref[:, 2 * gw:N_PF * gw] = _dot(h, w_ref[:, 5 * gw:12 * gw])


def _inproj(x2, g, w, l, tm):
    n, d = x2.shape
    d_in = w.shape[2]
    return pl.pallas_call(
        _inproj_kernel,
        grid=(n // tm,),
        in_specs=[
            pl.BlockSpec((tm, d), lambda i: (i, 0)),
            _resident((1, d)),
            _layer_resident(w, l),
        ],
        out_specs=[
            pl.BlockSpec((tm, N_PF * GROUP_WIDTH), lambda i: (i, 0)),
            pl.BlockSpec((tm, 3 * GROUP_WIDTH), lambda i: (i, 0)),
        ],
        out_shape=[
            jax.ShapeDtypeStruct((n, N_PF * GROUP_WIDTH), F32),
            jax.ShapeDtypeStruct((n, 3 * GROUP_WIDTH), BF16),
        ],
        compiler_params=pltpu.CompilerParams(
            dimension_semantics=("parallel",), vmem_limit_bytes=VMEM_LIMIT),
        name="inproj",
    )(x2, g, w)


def _convmix_kernel(a_ref, gate_ref, a_h_ref, gate_h_ref, scb_ref, scc_ref, sch_ref,
                    scc_h_ref, sch_h_ref, dww_ref, dwb_ref, lng_ref, lnb_ref, scw_ref,
                    yconf_ref, ysc_ref, hbuf, ubuf, hrot, *, tt, sub):
    t = pl.program_id(1)
    first = t == 0

    halo = a_h_ref[0] * jax.nn.sigmoid(gate_h_ref[0])
    hbuf[0:CONV_HALO, :] = jnp.where(first, 0.0, halo)
    hbuf[CONV_HALO:CONV_HALO + tt, :] = a_ref[0] * jax.nn.sigmoid(gate_ref[0])

    uhalo = scc_h_ref[0] * sch_h_ref[0]
    ubuf[0:SHORT_HALO, :] = jnp.where(first, 0.0, uhalo)
    ubuf[SHORT_HALO:SHORT_HALO + tt, :] = scc_ref[0] * sch_ref[0]

    n_rot = hrot.shape[1]
    for s in range(1, F32_SUBLANES):
        hrot[s - 1, :, :] = hbuf[s:s + n_rot, :]

    bias = dwb_ref[...]
    lng = lng_ref[...]
    lnb = lnb_ref[...]
    for r in range(tt // sub):
        base = r * sub
        acc = jnp.broadcast_to(bias, (sub, GROUP_WIDTH))
        for k in range(CONF_KERNEL):
            off = CONV_HALO - (CONF_KERNEL - 1) + k
            phase = off % F32_SUBLANES
            lo = off - phase + base
            if phase == 0:
                taps = hbuf[lo:lo + sub, :]
            else:
                taps = hrot[phase - 1, lo:lo + sub, :]
            acc = acc + dww_ref[k:k + 1, :] * taps
        mu = jnp.mean(acc, axis=-1, keepdims=True)
        xc = acc - mu
        var = jnp.mean(xc * xc, axis=-1, keepdims=True)
        y = xc * lax.rsqrt(var + NORM_EPS) * lng + lnb
        yconf_ref[0, base:base + sub, :] = (y * jax.nn.sigmoid(y)).astype(yconf_ref.dtype)

        conv = jnp.zeros((sub, GROUP_WIDTH), F32)
        for k in range(SHORT_KERNEL):
            off = SHORT_HALO - (SHORT_KERNEL - 1) + k + base
            conv = conv + scw_ref[k:k + 1, :] * ubuf[off:off + sub, :]
        ysc_ref[0, base:base + sub, :] = (scb_ref[0, base:base + sub, :] * conv).astype(ysc_ref.dtype)


def _convmix(pf3, dww, dwb, lng, lnb, scw, tt):
    b, t, _ = pf3.shape
    gw = GROUP_WIDTH
    sub = 128

    def main(col):
        return pl.BlockSpec((1, tt, gw), lambda bi, ti: (bi, ti, col))

    def halo(col, rows):
        per = tt // rows
        return pl.BlockSpec((1, rows, gw), lambda bi, ti: (bi, jnp.maximum(ti * per - 1, 0), col))

    kern = functools.partial(_convmix_kernel, tt=tt, sub=sub)
    return pl.pallas_call(
        kern,
        grid=(b, t // tt),
        in_specs=[
            main(PF_CONF_A), main(PF_CONF_GATE),
            halo(PF_CONF_A, CONV_HALO), halo(PF_CONF_GATE, CONV_HALO),
            main(PF_SC_B), main(PF_SC_C), main(PF_SC_H),
            halo(PF_SC_C, SHORT_HALO), halo(PF_SC_H, SHORT_HALO),
            _resident((CONF_KERNEL, gw)), _resident((1, gw)), _resident((1, gw)),
            _resident((1, gw)), _resident((SHORT_KERNEL, gw)),
        ],
        out_specs=[
            pl.BlockSpec((1, tt, gw), lambda bi, ti: (bi, ti, 0)),
            pl.BlockSpec((1, tt, gw), lambda bi, ti: (bi, ti, 0)),
        ],
        out_shape=[
            jax.ShapeDtypeStruct((b, t, gw), BF16),
            jax.ShapeDtypeStruct((b, t, gw), BF16),
        ],
        scratch_shapes=[
            pltpu.VMEM((CONV_HALO + tt, gw), F32),
            pltpu.VMEM((SHORT_HALO + tt, gw), F32),
            pltpu.VMEM((F32_SUBLANES - 1, CONV_HALO + tt - F32_SUBLANES, gw), F32),
        ],
        compiler_params=pltpu.CompilerParams(
            dimension_semantics=("parallel", "arbitrary"), vmem_limit_bytes=VMEM_LIMIT),
        name="convmix",
    )(pf3, pf3, pf3, pf3, pf3, pf3, pf3, pf3, pf3, dww, dwb, lng, lnb, scw)


def _sb_kernel(q_ref, k_ref, v_ref, o_ref, acc_ref, vh_ref, *, tk, n_kt):
    qi = pl.program_id(1)
    lane_head = lax.broadcasted_iota(jnp.int32, (1, GROUP_WIDTH), 1) // HEAD_DIM
    row = lax.broadcasted_iota(jnp.int32, (tk, tk), 0)
    col = lax.broadcasted_iota(jnp.int32, (tk, tk), 1)
    later = jnp.where(row > col, 1.0, 0.0).astype(BF16)
    causal = col < row
    in_head = [lane_head == h for h in range(N_HEADS)]

    @pl.when(qi == 0)
    def _():
        def fill(jt, _):
            vj = v_ref[0, pl.ds(pl.multiple_of(jt * tk, tk), tk), :]
            for h in range(N_HEADS):
                vh_ref[jt, h * tk:(h + 1) * tk, :] = jnp.where(in_head[h], vj, jnp.zeros_like(vj))
            return 0
        lax.fori_loop(0, n_kt, fill, 0)

    acc_ref[...] = jnp.zeros_like(acc_ref)
    q_sub = [q_ref[0, 0:tk, :], q_ref[0, tk:2 * tk, :]]
    qh = [[jnp.where(m, q, jnp.zeros_like(q)) for m in in_head] for q in q_sub]

    def block(j, runs, modes):
        kj = k_ref[0, pl.ds(pl.multiple_of(j * tk, tk), tk), :]
        new_runs = list(runs)
        for s in range(2):
            if modes[s] is None:
                continue
            diag = modes[s] == "diag"
            ps = []
            for h in range(N_HEADS):
                z = _dot_nt(qh[s][h], kj)
                lg = jnp.log(1.0 + jnp.exp2(jnp.abs(z) * (-LOG2E)))
                sp = jnp.maximum(z, 0.0) + lg
                if diag:
                    sp = jnp.where(causal, sp, 0.0)
                after = _dot(sp.astype(BF16), later)
                run = runs[s * N_HEADS + h]
                p = jnp.exp(z - sp - after - run)
                if diag:
                    p = jnp.where(causal, p, 0.0)
                ps.append(p.astype(BF16))
                new_runs[s * N_HEADS + h] = run + jnp.sum(sp, axis=1, keepdims=True)
            acc_ref[s * tk:(s + 1) * tk, :] += _dot(jnp.concatenate(ps, axis=1), vh_ref[j])
        return tuple(new_runs)

    zero = jnp.zeros((tk, 1), F32)
    runs = block(2 * qi + 1, (zero,) * (2 * N_HEADS), (None, "diag"))
    runs = block(2 * qi, runs, ("diag", "full"))

    both = ("full", "full")

    def quad(jj, c):
        j = 2 * qi - 1 - 4 * jj
        return block(j - 3, block(j - 2, block(j - 1, block(j, c, both), both), both), both)

    runs = lax.fori_loop(0, qi // 2, quad, runs)

    @pl.when(qi % 2 == 1)
    def _():
        block(0, block(1, runs, both), both)

    o_ref[0] = acc_ref[...].astype(o_ref.dtype)


def _stickbreak(qkv3, tk):
    b, t, _ = qkv3.shape
    gw = GROUP_WIDTH
    n_kt = t // tk
    tq = 2 * tk
    kern = functools.partial(_sb_kernel, tk=tk, n_kt=n_kt)
    return pl.pallas_call(
        kern,
        grid=(b, t // tq),
        in_specs=[
            pl.BlockSpec((1, tq, gw), lambda bi, qi: (bi, qi, 0)),
            pl.BlockSpec((1, t, gw), lambda bi, qi: (bi, 0, 1)),
            pl.BlockSpec((1, t, gw), lambda bi, qi: (bi, 0, 2)),
        ],
        out_specs=pl.BlockSpec((1, tq, gw), lambda bi, qi: (bi, qi, 0)),
        out_shape=jax.ShapeDtypeStruct((b, t, gw), BF16),
        scratch_shapes=[pltpu.VMEM((tq, gw), F32),
                        pltpu.VMEM((n_kt, N_HEADS * tk, gw), BF16)],
        compiler_params=pltpu.CompilerParams(
            dimension_semantics=("parallel", "arbitrary"), vmem_limit_bytes=VMEM_LIMIT),
        name="stickbreak",
    )(qkv3, qkv3, qkv3)


def _ret_kernel(q_ref, k_ref, v_ref, g_ref, cos_ref, sin_ref, dintra_ref, qdec_ref, kdec_ref,
                cdec_ref, ng_ref, o_ref, state_ref, *, tt):
    gw = GROUP_WIDTH
    c = RET_CHUNK

    @pl.when(pl.program_id(1) == 0)
    def _():
        state_ref[...] = jnp.zeros_like(state_ref)

    lane = lax.broadcasted_iota(jnp.int32, (1, gw), 1)
    lane_head = lane // HEAD_DIM
    first_half = (lane % HEAD_DIM) < (HEAD_DIM // 2)
    rr = lax.broadcasted_iota(jnp.int32, (gw, gw), 0) // HEAD_DIM
    cc = lax.broadcasted_iota(jnp.int32, (gw, gw), 1) // HEAD_DIM
    same_head = rr == cc
    head_mean = jnp.where(same_head, 1.0 / HEAD_DIM, 0.0).astype(BF16)

    def rot(x, cos, sin):
        swapped = jnp.where(first_half, pltpu.roll(x, gw - HEAD_DIM // 2, 1),
                            pltpu.roll(x, HEAD_DIM // 2, 1))
        return x * cos + swapped * sin

    def seg_mean(x):
        hi, lo = _split_bf16(x)
        return _dot(hi, head_mean) + _dot(lo, head_mean)

    for ci in range(tt // c):
        rows = slice(ci * c, (ci + 1) * c)
        cos = cos_ref[rows, :]
        sin = sin_ref[rows, :]
        qr = rot(q_ref[0, rows, :], cos, sin)
        kr = rot(k_ref[0, rows, :], cos, sin) * (HEAD_DIM ** -0.5)
        v = v_ref[0, rows, :]
        kb = kr.astype(BF16)
        vb = v.astype(BF16)
        state = state_ref[...]

        o = _dot(qr.astype(BF16), state.astype(BF16)) * qdec_ref[...]
        for h in range(N_HEADS):
            in_head = lane_head == h
            qh = jnp.where(in_head, qr, 0.0).astype(BF16)
            inner = _dot_nt(qh, kb) * dintra_ref[h]
            vh = jnp.where(in_head, vb, jnp.zeros_like(vb))
            o = o + _dot(inner.astype(BF16), vh)

        kd_t = (kr * kdec_ref[...]).T.astype(BF16)
        kv = _dot(kd_t, vb)
        state_ref[...] = state * cdec_ref[...] + jnp.where(same_head, kv, 0.0)

        mu = seg_mean(o)
        oc = o - mu
        var = seg_mean(oc * oc)
        y = oc * lax.rsqrt(var + NORM_EPS) * ng_ref[...]
        g = g_ref[0, rows, :]
        o_ref[0, rows, :] = (g * jax.nn.sigmoid(g) * y).astype(o_ref.dtype)


def _retention_tables(t):
    d = HEAD_DIM
    half = d // 2
    inv_freq = ROPE_BASE ** (-jnp.arange(0, d, 2, dtype=F32) / d)
    ang = jnp.arange(t).astype(F32)[:, None] * inv_freq[None, :]
    cos, sin = jnp.cos(ang), jnp.sin(ang)
    cos_h = jnp.concatenate([cos, cos], axis=-1)
    sin_h = jnp.concatenate([-sin, sin], axis=-1)
    cos_t = jnp.tile(cos_h, (1, N_HEADS))
    sin_t = jnp.tile(sin_h, (1, N_HEADS))

    log_gamma = jnp.log1p(-jnp.exp2(-5.0 - jnp.arange(N_HEADS, dtype=F32)))
    idx = jnp.arange(RET_CHUNK, dtype=F32)
    diff = idx[:, None] - idx[None, :]
    causal = diff >= 0
    d_intra = jnp.where(causal[None], jnp.exp(jnp.where(causal, diff, 0.0)[None] * log_gamma[:, None, None]), 0.0)
    q_decay = jnp.exp((idx[None, :] + 1.0) * log_gamma[:, None])
    k_decay = jnp.exp((RET_CHUNK - 1.0 - idx[None, :]) * log_gamma[:, None])
    chunk_decay = jnp.exp(RET_CHUNK * log_gamma)
    qdec = jnp.repeat(q_decay.T, d, axis=1)
    kdec = jnp.repeat(k_decay.T, d, axis=1)
    cdec = jnp.repeat(chunk_decay, d)[None, :]
    return cos_t, sin_t, d_intra, qdec, kdec, cdec


def _retention(pf3, tables, norm_g, tt):
    b, t, _ = pf3.shape
    gw = GROUP_WIDTH
    cos_t, sin_t, d_intra, qdec, kdec, cdec = tables

    def main(col):
        return pl.BlockSpec((1, tt, gw), lambda bi, ti: (bi, ti, col))

    kern = functools.partial(_ret_kernel, tt=tt)
    return pl.pallas_call(
        kern,
        grid=(b, t // tt),
        in_specs=[
            main(PF_RET_Q), main(PF_RET_K), main(PF_RET_V), main(PF_RET_G),
            pl.BlockSpec((tt, gw), lambda bi, ti: (ti, 0)),
            pl.BlockSpec((tt, gw), lambda bi, ti: (ti, 0)),
            _resident((N_HEADS, RET_CHUNK, RET_CHUNK)),
            _resident((RET_CHUNK, gw)), _resident((RET_CHUNK, gw)),
            _resident((1, gw)), _resident((1, gw)),
        ],
        out_specs=pl.BlockSpec((1, tt, gw), lambda bi, ti: (bi, ti, 0)),
        out_shape=jax.ShapeDtypeStruct((b, t, gw), BF16),
        scratch_shapes=[pltpu.VMEM((gw, gw), F32)],
        compiler_params=pltpu.CompilerParams(
            dimension_semantics=("parallel", "arbitrary"), vmem_limit_bytes=VMEM_LIMIT),
        name="retention",
    )(pf3, pf3, pf3, pf3, cos_t, sin_t, d_intra, qdec, kdec, cdec, norm_g)


def _outproj_kernel(yc_ref, ysb_ref, yr_ref, ysc_ref, w_ref, g_ref, x_ref, o_ref):
    gw = GROUP_WIDTH
    mix = _dot(yc_ref[...], w_ref[0:gw, :])
    mix = mix + _dot(ysb_ref[...], w_ref[gw:2 * gw, :])
    mix = mix + _dot(yr_ref[...], w_ref[2 * gw:3 * gw, :])
    mix = mix + _dot(ysc_ref[...], w_ref[3 * gw:4 * gw, :])
    ms = jnp.mean(mix * mix, axis=-1, keepdims=True)
    o_ref[...] = x_ref[...] + mix * lax.rsqrt(ms + NORM_EPS) * g_ref[...]


def _outproj(ys, w, l, g, x2, tm):
    n, d = x2.shape
    gw = GROUP_WIDTH
    yspec = pl.BlockSpec((tm, gw), lambda i: (i, 0))
    return pl.pallas_call(
        _outproj_kernel,
        grid=(n // tm,),
        in_specs=[yspec, yspec, yspec, yspec, _layer_resident(w, l), _resident((1, d)),
                  pl.BlockSpec((tm, d), lambda i: (i, 0))],
        out_specs=pl.BlockSpec((tm, d), lambda i: (i, 0)),
        out_shape=jax.ShapeDtypeStruct((n, d), F32),
        compiler_params=pltpu.CompilerParams(
            dimension_semantics=("parallel",), vmem_limit_bytes=VMEM_LIMIT),
        name="outproj",
    )(*ys, w, g, x2)


def _ffn_kernel(x_ref, xh_ref, gpre_ref, gpost_ref, wup_ref, wconv_ref, wdown_ref, o_ref,
                xp_ref, xn_ref, hga_ref, hua_ref, hgb_ref, hub_ref, f_ref, *, tm, tiles_per_seq, n_chunks):
    assert FFN_KERNEL == 3
    i = pl.program_id(0)
    seq_start = (i % tiles_per_seq) == 0
    gpre = gpre_ref[...]
    d = x_ref.shape[1]
    s8 = F32_SUBLANES
    seg = tm // s8

    def norm(x):
        ms = jnp.mean(x * x, axis=-1, keepdims=True)
        return x * lax.rsqrt(ms + NORM_EPS) * gpre

    xp_ref[...] = pltpu.einshape("sgd->gsd", x_ref[...].reshape(s8, seg, d)).reshape(tm, d)
    xn_ref[0:tm, :] = norm(xp_ref[...]).astype(BF16)
    xn_ref[tm:tm + FFN_HALO, :] = jnp.where(seq_start, 0.0, norm(xh_ref[...])).astype(BF16)
    f_ref[...] = jnp.zeros_like(f_ref)

    fc = wdown_ref.shape[1]
    first_sublane = lax.broadcasted_iota(jnp.int32, (s8, fc), 0) == 0

    def conv(h_ref, w):
        w0, w1, w2 = w[0:1, :], w[1:2, :], w[2:3, :]
        main = w2 * h_ref[2 * s8:tm, :] + w1 * h_ref[s8:tm - s8, :] + w0 * h_ref[0:tm - 2 * s8, :]
        prev1 = jnp.where(first_sublane, h_ref[tm + FFN_HALO - 1:tm + FFN_HALO, :],
                          pltpu.roll(h_ref[tm - s8:tm, :], 1, 0))
        prev2 = jnp.where(first_sublane, h_ref[tm + FFN_HALO - 2:tm + FFN_HALO - 1, :],
                          pltpu.roll(h_ref[tm - 2 * s8:tm - s8, :], 1, 0))
        g0 = w2 * h_ref[0:s8, :] + w1 * prev1 + w0 * prev2
        g1 = w2 * h_ref[s8:2 * s8, :] + w1 * h_ref[0:s8, :] + w0 * prev1
        return jnp.concatenate([g0, g1, main], axis=0)

    def cols(c):
        return pl.ds(pl.multiple_of(c * fc, fc), fc)

    def up_proj(c, h_refs):
        xn = xn_ref[...]
        h_refs[0][...] = _dot(xn, wup_ref[:, cols(c)])
        h_refs[1][...] = _dot(xn, wup_ref[:, cols(n_chunks + c)])

    def down_proj(c, h_refs):
        gate = conv(h_refs[0], wconv_ref[:, cols(c)])
        up = conv(h_refs[1], wconv_ref[:, cols(n_chunks + c)])
        act = (gate * jax.nn.sigmoid(gate) * up).astype(BF16)
        f_ref[...] += _dot(act, wdown_ref[c])

    slot_a = (hga_ref, hua_ref)
    slot_b = (hgb_ref, hub_ref)
    assert n_chunks % 2 == 1
    up_proj(0, slot_a)

    def chunk_pair(cp, _):
        c = 2 * cp
        up_proj(c + 1, slot_b)
        down_proj(c, slot_a)
        up_proj(c + 2, slot_a)
        down_proj(c + 1, slot_b)
        return 0

    lax.fori_loop(0, n_chunks // 2, chunk_pair, 0)
    down_proj(n_chunks - 1, slot_a)
    f = f_ref[...]
    ms = jnp.mean(f * f, axis=-1, keepdims=True)
    out = xp_ref[...] + f * lax.rsqrt(ms + NORM_EPS) * gpost_ref[...]
    o_ref[...] = pltpu.einshape("gsd->sgd", out.reshape(seg, s8, d)).reshape(tm, d)


def _ffn(x2, gpre, gpost, wup, wconv, wdown, l, tm, seq):
    n, d = x2.shape
    n_chunks = wdown.shape[1]
    fc = wdown.shape[2]
    per = tm // FFN_HALO
    kern = functools.partial(_ffn_kernel, tm=tm, tiles_per_seq=seq // tm, n_chunks=n_chunks)
    return pl.pallas_call(
        kern,
        grid=(n // tm,),
        in_specs=[
            pl.BlockSpec((tm, d), lambda i: (i, 0)),
            pl.BlockSpec((FFN_HALO, d), lambda i: (jnp.maximum(i * per - 1, 0), 0)),
            _resident((1, d)), _resident((1, d)),
            _layer_resident(wup, l), _layer_resident(wconv, l), _layer_resident(wdown, l),
        ],
        out_specs=pl.BlockSpec((tm, d), lambda i: (i, 0)),
        out_shape=jax.ShapeDtypeStruct((n, d), F32),
        scratch_shapes=[
            pltpu.VMEM((tm, d), F32),
            pltpu.VMEM((FFN_HALO + tm, d), BF16),
            pltpu.VMEM((FFN_HALO + tm, fc), F32),
            pltpu.VMEM((FFN_HALO + tm, fc), F32),
            pltpu.VMEM((FFN_HALO + tm, fc), F32),
            pltpu.VMEM((FFN_HALO + tm, fc), F32),
            pltpu.VMEM((tm, d), F32),
        ],
        compiler_params=pltpu.CompilerParams(
            dimension_semantics=("parallel",), vmem_limit_bytes=VMEM_LIMIT),
        name="convffn",
    )(x2, x2, gpre, gpost, wup, wconv, wdown)


def kernel(x, norm_mix_pre, norm_mix_post, norm_ffn_pre, norm_ffn_post, w_in, conf_dw_w, conf_dw_b, conf_ln_g, conf_ln_b, ret_norm_g, sc_conv_w, w_out, ffn_up, ffn_conv_w, ffn_down):
    b, t, d = x.shape
    depth = w_in.shape[0]
    d_ff = ffn_down.shape[1]
    n = b * t
    gw = GROUP_WIDTH
    assert w_in.shape[2] == 12 * gw and w_out.shape[1] == 4 * gw
    assert d_ff % FFN_CHUNK == 0
    n_chunks = d_ff // FFN_CHUNK

    tm_in = min(1024, t)
    tm_out = min(1024, t)
    tm_ffn = min(512, t)
    tt_conv = min(1024, t)
    tt_ret = min(2048, t)
    tk = min(256, t // 2)
    assert t % tm_in == 0 and t % tt_ret == 0 and t % (2 * tk) == 0 and t % RET_CHUNK == 0

    tables = _retention_tables(t)
    w_in_b = w_in.astype(BF16)
    w_out_b = w_out.astype(BF16)
    wup_b = ffn_up.astype(BF16)
    wdown_b = ffn_down.astype(BF16).reshape(depth, n_chunks, FFN_CHUNK, d)
    x2 = x.reshape(n, d)
    for l in range(depth):
        row = lambda a: a[l][None, :]
        pf, qkv = _inproj(x2, row(norm_mix_pre), w_in_b, l, tm_in)
        pf3 = pf.reshape(b, t, N_PF * gw)
        qkv3 = qkv.reshape(b, t, 3 * gw)
        y_conf, y_sc = _convmix(pf3, conf_dw_w[l], row(conf_dw_b), row(conf_ln_g), row(conf_ln_b),
                                sc_conv_w[l], tt_conv)
        y_sb = _stickbreak(qkv3, tk)
        y_ret = _retention(pf3, tables, row(ret_norm_g), tt_ret)
        ys = [y.reshape(n, gw) for y in (y_conf, y_sb, y_ret, y_sc)]
        x2 = _outproj(ys, w_out_b, l, row(norm_mix_post), x2, tm_out)
        x2 = _ffn(x2, row(norm_ffn_pre), row(norm_ffn_post), wup_b, ffn_conv_w, wdown_b, l, tm_ffn, t)
    return x2.reshape(b, t, d)
```

```python
import functools

import jax
import jax.numpy as jnp
from jax import lax
from jax.experimental import pallas as pl
from jax.experimental.pallas import tpu as pltpu

F32 = jnp.float32
BF16 = jnp.bfloat16

GROUP_WIDTH = 256
HEAD_DIM = 64
N_HEADS = GROUP_WIDTH // HEAD_DIM
CONF_KERNEL = 31
SHORT_KERNEL = 3
FFN_KERNEL = 3
RET_CHUNK = 128
ROPE_BASE = 10000.0
NORM_EPS = 1e-6
LOG2E = 1.4426950408889634

PF_RET_Q, PF_RET_K, PF_RET_V, PF_RET_G = range(4)
N_PF = 4

VMEM_LIMIT = 56 * 1024 * 1024
F32_SUBLANES = 8

CONV_HALO = 32
SHORT_HALO = 8
FFN_HALO = 16
FFN_CHUNK = 256


def _layer_resident(stack, l):
    shape = stack.shape[1:]
    return pl.BlockSpec((None,) + shape, lambda *_: (l,) + (0,) * len(shape),
                        pipeline_mode=pl.Buffered(1))


def _dot(a, b):
    return jnp.dot(a, b, preferred_element_type=F32)


def _dot_nt(a, b):
    return lax.dot_general(a, b, (((1,), (1,)), ((), ())), preferred_element_type=F32)


def _split_bf16(x):
    hi = x.astype(BF16)
    lo = (x - hi.astype(F32)).astype(BF16)
    return hi, lo


def _resident(shape):
    nd = len(shape)
    return pl.BlockSpec(shape, lambda *_: (0,) * nd, pipeline_mode=pl.Buffered(1))


def _inmix_kernel(x_ref, g_ref, w_ref, dww_ref, dwb_ref, lng_ref, lnb_ref, scw_ref,
                  pr_ref, qkv_ref, yconf_ref, ysc_ref,
                  stage_ref, scb_ref, hbuf, ubuf, hrot, htail, utail, *, tm, sub, tiles_per_seq):
    i = pl.program_id(0)
    gw = GROUP_WIDTH

    @pl.when(i == 0)
    def _():
        stage_ref[...] = jnp.zeros_like(stage_ref)
        htail[...] = jnp.zeros_like(htail)
        utail[...] = jnp.zeros_like(utail)

    first = ((i + tiles_per_seq - 1) % tiles_per_seq) == 0
    hbuf[0:CONV_HALO, :] = jnp.where(first, 0.0, htail[...])
    hbuf[CONV_HALO:CONV_HALO + tm, :] = stage_ref[:, 0:gw] * jax.nn.sigmoid(stage_ref[:, gw:2 * gw])
    ubuf[0:SHORT_HALO, :] = jnp.where(first, 0.0, utail[...])
    ubuf[SHORT_HALO:SHORT_HALO + tm, :] = stage_ref[:, 3 * gw:4 * gw] * stage_ref[:, 4 * gw:5 * gw]
    scb_ref[...] = stage_ref[:, 2 * gw:3 * gw]
    htail[...] = hbuf[tm:tm + CONV_HALO, :]
    utail[...] = ubuf[tm:tm + SHORT_HALO, :]

    n_rot = hrot.shape[1]
    for s in range(1, F32_SUBLANES):
        hrot[s - 1, :, :] = hbuf[s:s + n_rot, :]

    bias = dwb_ref[...]
    lng = lng_ref[...]
    lnb = lnb_ref[...]
    for r in range(tm // sub):
        base = r * sub
        acc = jnp.broadcast_to(bias, (sub, gw))
        for k in range(CONF_KERNEL):
            off = CONV_HALO - (CONF_KERNEL - 1) + k
            phase = off % F32_SUBLANES
            lo = off - phase + base
            if phase == 0:
                taps = hbuf[lo:lo + sub, :]
            else:
                taps = hrot[phase - 1, lo:lo + sub, :]
            acc = acc + dww_ref[k:k + 1, :] * taps
        mu = jnp.mean(acc, axis=-1, keepdims=True)
        xc = acc - mu
        var = jnp.mean(xc * xc, axis=-1, keepdims=True)
        y = xc * lax.rsqrt(var + NORM_EPS) * lng + lnb
        yconf_ref[base:base + sub, :] = (y * jax.nn.sigmoid(y)).astype(yconf_ref.dtype)

        conv = jnp.zeros((sub, gw), F32)
        for k in range(SHORT_KERNEL):
            off = SHORT_HALO - (SHORT_KERNEL - 1) + k + base
            conv = conv + scw_ref[k:k + 1, :] * ubuf[off:off + sub, :]
        ysc_ref[base:base + sub, :] = (scb_ref[base:base + sub, :] * conv).astype(ysc_ref.dtype)

    x = x_ref[...]
    ms = jnp.mean(x * x, axis=-1, keepdims=True)
    h = (x * lax.rsqrt(ms + NORM_EPS) * g_ref[...]).astype(BF16)
    qkv_ref[:, 0:gw] = (_dot(h, w_ref[:, 2 * gw:3 * gw]) * (HEAD_DIM ** -0.5)).astype(BF16)
    qkv_ref[:, gw:3 * gw] = _dot(h, w_ref[:, 3 * gw:5 * gw]).astype(BF16)
    pr_ref[...] = _dot(h, w_ref[:, 5 * gw:9 * gw])
    stage_ref[:, 0:2 * gw] = _dot(h, w_ref[:, 0:2 * gw])
    stage_ref[:, 2 * gw:5 * gw] = _dot(h, w_ref[:, 9 * gw:12 * gw])


def _inmix(x2, g, w, l, dww, dwb, lng, lnb, scw, tm, seq):
    n, d = x2.shape
    gw = GROUP_WIDTH
    n_tiles = n // tm
    sub = 32
    kern = functools.partial(_inmix_kernel, tm=tm, sub=sub, tiles_per_seq=seq // tm)
    cur = lambda i: (jnp.minimum(i, n_tiles - 1), 0)
    prev = lambda i: (jnp.maximum(i - 1, 0), 0)
    return pl.pallas_call(
        kern,
        grid=(n_tiles + 1,),
        in_specs=[
            pl.BlockSpec((tm, d), cur),
            _resident((1, d)),
            _layer_resident(w, l),
            _resident((CONF_KERNEL, gw)), _resident((1, gw)), _resident((1, gw)),
            _resident((1, gw)), _resident((SHORT_KERNEL, gw)),
        ],
        out_specs=[
            pl.BlockSpec((tm, 4 * gw), cur),
            pl.BlockSpec((tm, 3 * gw), cur),
            pl.BlockSpec((tm, gw), prev),
            pl.BlockSpec((tm, gw), prev),
        ],
        out_shape=[
            jax.ShapeDtypeStruct((n, 4 * gw), F32),
            jax.ShapeDtypeStruct((n, 3 * gw), BF16),
            jax.ShapeDtypeStruct((n, gw), BF16),
            jax.ShapeDtypeStruct((n, gw), BF16),
        ],
        scratch_shapes=[
            pltpu.VMEM((tm, 5 * gw), F32),
            pltpu.VMEM((tm, gw), F32),
            pltpu.VMEM((CONV_HALO + tm, gw), F32),
            pltpu.VMEM((SHORT_HALO + tm, gw), F32),
            pltpu.VMEM((F32_SUBLANES - 1, CONV_HALO + tm - F32_SUBLANES, gw), F32),
            pltpu.VMEM((CONV_HALO, gw), F32),
            pltpu.VMEM((SHORT_HALO, gw), F32),
        ],
        compiler_params=pltpu.CompilerParams(
            dimension_semantics=("arbitrary",), vmem_limit_bytes=VMEM_LIMIT),
        name="inmix",
    )(x2, g, w, dww, dwb, lng, lnb, scw)


def _sb_kernel(q_ref, k_ref, v_ref, o_ref, acc_ref, vh_ref, *, tk, n_kt):
    qi = pl.program_id(1)
    lane_head = lax.broadcasted_iota(jnp.int32, (1, GROUP_WIDTH), 1) // HEAD_DIM
    row = lax.broadcasted_iota(jnp.int32, (tk, tk), 0)
    col = lax.broadcasted_iota(jnp.int32, (tk, tk), 1)
    later = jnp.where(row > col, 1.0, 0.0).astype(BF16)
    causal = col < row
    in_head = [lane_head == h for h in range(N_HEADS)]

    @pl.when(qi == 0)
    def _():
        def fill(jt, _):
            vj = v_ref[0, pl.ds(pl.multiple_of(jt * tk, tk), tk), :]
            for h in range(N_HEADS):
                vh_ref[jt, h * tk:(h + 1) * tk, :] = jnp.where(in_head[h], vj, jnp.zeros_like(vj))
            return 0
        lax.fori_loop(0, n_kt, fill, 0)

    acc_ref[...] = jnp.zeros_like(acc_ref)
    q_sub = [q_ref[0, 0:tk, :], q_ref[0, tk:2 * tk, :]]
    qh = [[jnp.where(m, q, jnp.zeros_like(q)) for m in in_head] for q in q_sub]

    def block(j, runs, modes):
        kj = k_ref[0, pl.ds(pl.multiple_of(j * tk, tk), tk), :]
        new_runs = list(runs)
        for s in range(2):
            if modes[s] is None:
                continue
            diag = modes[s] == "diag"
            ps = []
            for h in range(N_HEADS):
                z = _dot_nt(qh[s][h], kj)
                lg = jnp.log(1.0 + jnp.exp2(jnp.abs(z) * (-LOG2E)))
                sp = jnp.maximum(z, 0.0) + lg
                if diag:
                    sp = jnp.where(causal, sp, 0.0)
                after = _dot(sp.astype(BF16), later)
                run = runs[s * N_HEADS + h]
                p = jnp.exp(z - sp - after - run)
                if diag:
                    p = jnp.where(causal, p, 0.0)
                ps.append(p.astype(BF16))
                new_runs[s * N_HEADS + h] = run + jnp.sum(sp, axis=1, keepdims=True)
            acc_ref[s * tk:(s + 1) * tk, :] += _dot(jnp.concatenate(ps, axis=1), vh_ref[j])
        return tuple(new_runs)

    zero = jnp.zeros((tk, 1), F32)
    runs = block(2 * qi + 1, (zero,) * (2 * N_HEADS), (None, "diag"))
    runs = block(2 * qi, runs, ("diag", "full"))

    both = ("full", "full")

    def quad(jj, c):
        j = 2 * qi - 1 - 4 * jj
        return block(j - 3, block(j - 2, block(j - 1, block(j, c, both), both), both), both)

    runs = lax.fori_loop(0, qi // 2, quad, runs)

    @pl.when(qi % 2 == 1)
    def _():
        block(0, block(1, runs, both), both)

    o_ref[0] = acc_ref[...].astype(o_ref.dtype)


def _stickbreak(qkv3, tk):
    b, t, _ = qkv3.shape
    gw = GROUP_WIDTH
    n_kt = t // tk
    tq = 2 * tk
    kern = functools.partial(_sb_kernel, tk=tk, n_kt=n_kt)
    return pl.pallas_call(
        kern,
        grid=(b, t // tq),
        in_specs=[
            pl.BlockSpec((1, tq, gw), lambda bi, qi: (bi, qi, 0)),
            pl.BlockSpec((1, t, gw), lambda bi, qi: (bi, 0, 1)),
            pl.BlockSpec((1, t, gw), lambda bi, qi: (bi, 0, 2)),
        ],
        out_specs=pl.BlockSpec((1, tq, gw), lambda bi, qi: (bi, qi, 0)),
        out_shape=jax.ShapeDtypeStruct((b, t, gw), BF16),
        scratch_shapes=[pltpu.VMEM((tq, gw), F32),
                        pltpu.VMEM((n_kt, N_HEADS * tk, gw), BF16)],
        compiler_params=pltpu.CompilerParams(
            dimension_semantics=("parallel", "arbitrary"), vmem_limit_bytes=VMEM_LIMIT),
        name="stickbreak",
    )(qkv3, qkv3, qkv3)


def _ret_kernel(q_ref, k_ref, v_ref, g_ref, cos_ref, sin_ref, dintra_ref, qdec_ref, kdec_ref,
                cdec_ref, ng_ref, o_ref, state_ref, *, tt):
    gw = GROUP_WIDTH
    c = RET_CHUNK

    @pl.when(pl.program_id(1) == 0)
    def _():
        state_ref[...] = jnp.zeros_like(state_ref)

    lane = lax.broadcasted_iota(jnp.int32, (1, gw), 1)
    lane_head = lane // HEAD_DIM
    first_half = (lane % HEAD_DIM) < (HEAD_DIM // 2)
    rr = lax.broadcasted_iota(jnp.int32, (gw, gw), 0) // HEAD_DIM
    cc = lax.broadcasted_iota(jnp.int32, (gw, gw), 1) // HEAD_DIM
    same_head = rr == cc
    head_mean = jnp.where(same_head, 1.0 / HEAD_DIM, 0.0).astype(BF16)

    def rot(x, cos, sin):
        swapped = jnp.where(first_half, pltpu.roll(x, gw - HEAD_DIM // 2, 1),
                            pltpu.roll(x, HEAD_DIM // 2, 1))
        return x * cos + swapped * sin

    def seg_mean(x):
        hi, lo = _split_bf16(x)
        return _dot(hi, head_mean) + _dot(lo, head_mean)

    for ci in range(tt // c):
        rows = slice(ci * c, (ci + 1) * c)
        cos = cos_ref[rows, :]
        sin = sin_ref[rows, :]
        qr = rot(q_ref[0, rows, :], cos, sin)
        kr = rot(k_ref[0, rows, :], cos, sin) * (HEAD_DIM ** -0.5)
        v = v_ref[0, rows, :]
        kb = kr.astype(BF16)
        vb = v.astype(BF16)
        state = state_ref[...]

        o = _dot(qr.astype(BF16), state.astype(BF16)) * qdec_ref[...]
        for h in range(N_HEADS):
            in_head = lane_head == h
            qh = jnp.where(in_head, qr, 0.0).astype(BF16)
            inner = _dot_nt(qh, kb) * dintra_ref[h]
            vh = jnp.where(in_head, vb, jnp.zeros_like(vb))
            o = o + _dot(inner.astype(BF16), vh)

        kd_t = (kr * kdec_ref[...]).T.astype(BF16)
        kv = _dot(kd_t, vb)
        state_ref[...] = state * cdec_ref[...] + jnp.where(same_head, kv, 0.0)

        mu = seg_mean(o)
        oc = o - mu
        var = seg_mean(oc * oc)
        y = oc * lax.rsqrt(var + NORM_EPS) * ng_ref[...]
        g = g_ref[0, rows, :]
        o_ref[0, rows, :] = (g * jax.nn.sigmoid(g) * y).astype(o_ref.dtype)


def _retention_tables(t):
    d = HEAD_DIM
    half = d // 2
    inv_freq = ROPE_BASE ** (-jnp.arange(0, d, 2, dtype=F32) / d)
    ang = jnp.arange(t).astype(F32)[:, None] * inv_freq[None, :]
    cos, sin = jnp.cos(ang), jnp.sin(ang)
    cos_h = jnp.concatenate([cos, cos], axis=-1)
    sin_h = jnp.concatenate([-sin, sin], axis=-1)
    cos_t = jnp.tile(cos_h, (1, N_HEADS))
    sin_t = jnp.tile(sin_h, (1, N_HEADS))

    log_gamma = jnp.log1p(-jnp.exp2(-5.0 - jnp.arange(N_HEADS, dtype=F32)))
    idx = jnp.arange(RET_CHUNK, dtype=F32)
    diff = idx[:, None] - idx[None, :]
    causal = diff >= 0
    d_intra = jnp.where(causal[None], jnp.exp(jnp.where(causal, diff, 0.0)[None] * log_gamma[:, None, None]), 0.0)
    q_decay = jnp.exp((idx[None, :] + 1.0) * log_gamma[:, None])
    k_decay = jnp.exp((RET_CHUNK - 1.0 - idx[None, :]) * log_gamma[:, None])
    chunk_decay = jnp.exp(RET_CHUNK * log_gamma)
    qdec = jnp.repeat(q_decay.T, d, axis=1)
    kdec = jnp.repeat(k_decay.T, d, axis=1)
    cdec = jnp.repeat(chunk_decay, d)[None, :]
    return cos_t, sin_t, d_intra, qdec, kdec, cdec


def _retention(pf3, tables, norm_g, tt):
    b, t, _ = pf3.shape
    gw = GROUP_WIDTH
    cos_t, sin_t, d_intra, qdec, kdec, cdec = tables

    def main(col):
        return pl.BlockSpec((1, tt, gw), lambda bi, ti: (bi, ti, col))

    kern = functools.partial(_ret_kernel, tt=tt)
    return pl.pallas_call(
        kern,
        grid=(b, t // tt),
        in_specs=[
            main(PF_RET_Q), main(PF_RET_K), main(PF_RET_V), main(PF_RET_G),
            pl.BlockSpec((tt, gw), lambda bi, ti: (ti, 0)),
            pl.BlockSpec((tt, gw), lambda bi, ti: (ti, 0)),
            _resident((N_HEADS, RET_CHUNK, RET_CHUNK)),
            _resident((RET_CHUNK, gw)), _resident((RET_CHUNK, gw)),
            _resident((1, gw)), _resident((1, gw)),
        ],
        out_specs=pl.BlockSpec((1, tt, gw), lambda bi, ti: (bi, ti, 0)),
        out_shape=jax.ShapeDtypeStruct((b, t, gw), BF16),
        scratch_shapes=[pltpu.VMEM((gw, gw), F32)],
        compiler_params=pltpu.CompilerParams(
            dimension_semantics=("parallel", "arbitrary"), vmem_limit_bytes=VMEM_LIMIT),
        name="retention",
    )(pf3, pf3, pf3, pf3, cos_t, sin_t, d_intra, qdec, kdec, cdec, norm_g)


def _outproj_kernel(yc_ref, ysb_ref, yr_ref, ysc_ref, w_ref, g_ref, x_ref, o_ref):
    gw = GROUP_WIDTH
    mix = _dot(yc_ref[...], w_ref[0:gw, :])
    mix = mix + _dot(ysb_ref[...], w_ref[gw:2 * gw, :])
    mix = mix + _dot(yr_ref[...], w_ref[2 * gw:3 * gw, :])
    mix = mix + _dot(ysc_ref[...], w_ref[3 * gw:4 * gw, :])
    ms = jnp.mean(mix * mix, axis=-1, keepdims=True)
    o_ref[...] = x_ref[...] + mix * lax.rsqrt(ms + NORM_EPS) * g_ref[...]


def _outproj(ys, w, l, g, x2, tm):
    n, d = x2.shape
    gw = GROUP_WIDTH
    yspec = pl.BlockSpec((tm, gw), lambda i: (i, 0))
    return pl.pallas_call(
        _outproj_kernel,
        grid=(n // tm,),
        in_specs=[yspec, yspec, yspec, yspec, _layer_resident(w, l), _resident((1, d)),
                  pl.BlockSpec((tm, d), lambda i: (i, 0))],
        out_specs=pl.BlockSpec((tm, d), lambda i: (i, 0)),
        out_shape=jax.ShapeDtypeStruct((n, d), F32),
        compiler_params=pltpu.CompilerParams(
            dimension_semantics=("parallel",), vmem_limit_bytes=VMEM_LIMIT),
        name="outproj",
    )(*ys, w, g, x2)


def _ffn_kernel(x_ref, xh_ref, gpre_ref, gpost_ref, wup_ref, wconv_ref, wdown_ref, o_ref,
                xp_ref, xn_ref, hga_ref, hua_ref, hgb_ref, hub_ref, f_ref, *, tm, tiles_per_seq, n_chunks):
    assert FFN_KERNEL == 3
    i = pl.program_id(0)
    seq_start = (i % tiles_per_seq) == 0
    gpre = gpre_ref[...]
    d = x_ref.shape[1]
    s8 = F32_SUBLANES
    seg = tm // s8

    def norm(x):
        ms = jnp.mean(x * x, axis=-1, keepdims=True)
        return x * lax.rsqrt(ms + NORM_EPS) * gpre

    xp_ref[...] = pltpu.einshape("sgd->gsd", x_ref[...].reshape(s8, seg, d)).reshape(tm, d)
    xn_ref[0:tm, :] = norm(xp_ref[...]).astype(BF16)
    xn_ref[tm:tm + FFN_HALO, :] = jnp.where(seq_start, 0.0, norm(xh_ref[...])).astype(BF16)
    f_ref[...] = jnp.zeros_like(f_ref)

    fc = wdown_ref.shape[1]
    first_sublane = lax.broadcasted_iota(jnp.int32, (s8, fc), 0) == 0

    def conv(h_ref, w):
        w0, w1, w2 = w[0:1, :], w[1:2, :], w[2:3, :]
        main = w2 * h_ref[2 * s8:tm, :] + w1 * h_ref[s8:tm - s8, :] + w0 * h_ref[0:tm - 2 * s8, :]
        prev1 = jnp.where(first_sublane, h_ref[tm + FFN_HALO - 1:tm + FFN_HALO, :],
                          pltpu.roll(h_ref[tm - s8:tm, :], 1, 0))
        prev2 = jnp.where(first_sublane, h_ref[tm + FFN_HALO - 2:tm + FFN_HALO - 1, :],
                          pltpu.roll(h_ref[tm - 2 * s8:tm - s8, :], 1, 0))
        g0 = w2 * h_ref[0:s8, :] + w1 * prev1 + w0 * prev2
        g1 = w2 * h_ref[s8:2 * s8, :] + w1 * h_ref[0:s8, :] + w0 * prev1
        return jnp.concatenate([g0, g1, main], axis=0)

    def cols(c):
        return pl.ds(pl.multiple_of(c * fc, fc), fc)

    def up_proj(c, h_refs):
        xn = xn_ref[...]
        h_refs[0][...] = _dot(xn, wup_ref[:, cols(c)])
        h_refs[1][...] = _dot(xn, wup_ref[:, cols(n_chunks + c)])

    def down_proj(c, h_refs):
        gate = conv(h_refs[0], wconv_ref[:, cols(c)])
        up = conv(h_refs[1], wconv_ref[:, cols(n_chunks + c)])
        act = (gate * jax.nn.sigmoid(gate) * up).astype(BF16)
        f_ref[...] += _dot(act, wdown_ref[c])

    slot_a = (hga_ref, hua_ref)
    slot_b = (hgb_ref, hub_ref)
    assert n_chunks % 2 == 1
    up_proj(0, slot_a)

    def chunk_pair(cp, _):
        c = 2 * cp
        up_proj(c + 1, slot_b)
        down_proj(c, slot_a)
        up_proj(c + 2, slot_a)
        down_proj(c + 1, slot_b)
        return 0

    lax.fori_loop(0, n_chunks // 2, chunk_pair, 0)
    down_proj(n_chunks - 1, slot_a)
    f = f_ref[...]
    ms = jnp.mean(f * f, axis=-1, keepdims=True)
    out = xp_ref[...] + f * lax.rsqrt(ms + NORM_EPS) * gpost_ref[...]
    o_ref[...] = pltpu.einshape("gsd->sgd", out.reshape(seg, s8, d)).reshape(tm, d)


def _ffn(x2, gpre, gpost, wup, wconv, wdown, l, tm, seq):
    n, d = x2.shape
    n_chunks = wdown.shape[1]
    fc = wdown.shape[2]
    per = tm // FFN_HALO
    kern = functools.partial(_ffn_kernel, tm=tm, tiles_per_seq=seq // tm, n_chunks=n_chunks)
    return pl.pallas_call(
        kern,
        grid=(n // tm,),
        in_specs=[
            pl.BlockSpec((tm, d), lambda i: (i, 0)),
            pl.BlockSpec((FFN_HALO, d), lambda i: (jnp.maximum(i * per - 1, 0), 0)),
            _resident((1, d)), _resident((1, d)),
            _layer_resident(wup, l), _layer_resident(wconv, l), _layer_resident(wdown, l),
        ],
        out_specs=pl.BlockSpec((tm, d), lambda i: (i, 0)),
        out_shape=jax.ShapeDtypeStruct((n, d), F32),
        scratch_shapes=[
            pltpu.VMEM((tm, d), F32),
            pltpu.VMEM((FFN_HALO + tm, d), BF16),
            pltpu.VMEM((FFN_HALO + tm, fc), F32),
            pltpu.VMEM((FFN_HALO + tm, fc), F32),
            pltpu.VMEM((FFN_HALO + tm, fc), F32),
            pltpu.VMEM((FFN_HALO + tm, fc), F32),
            pltpu.VMEM((tm, d), F32),
        ],
        compiler_params=pltpu.CompilerParams(
            dimension_semantics=("parallel",), vmem_limit_bytes=VMEM_LIMIT),
        name="convffn",
    )(x2, x2, gpre, gpost, wup, wconv, wdown)


def kernel(x, norm_mix_pre, norm_mix_post, norm_ffn_pre, norm_ffn_post, w_in, conf_dw_w, conf_dw_b, conf_ln_g, conf_ln_b, ret_norm_g, sc_conv_w, w_out, ffn_up, ffn_conv_w, ffn_down):
    b, t, d = x.shape
    depth = w_in.shape[0]
    d_ff = ffn_down.shape[1]
    n = b * t
    gw = GROUP_WIDTH
    assert w_in.shape[2] == 12 * gw and w_out.shape[1] == 4 * gw
    assert d_ff % FFN_CHUNK == 0
    n_chunks = d_ff // FFN_CHUNK

    tm_in = min(1024, t)
    tm_out = min(1024, t)
    tm_ffn = min(512, t)
    tt_ret = min(2048, t)
    tk = min(256, t // 2)
    assert t % tm_in == 0 and t % tt_ret == 0 and t % (2 * tk) == 0 and t % RET_CHUNK == 0

    tables = _retention_tables(t)
    w_in_b = w_in.astype(BF16)
    w_out_b = w_out.astype(BF16)
    wup_b = ffn_up.astype(BF16)
    wdown_b = ffn_down.astype(BF16).reshape(depth, n_chunks, FFN_CHUNK, d)
    x2 = x.reshape(n, d)
    for l in range(depth):
        row = lambda a: a[l][None, :]
        pf, qkv, y_conf, y_sc = _inmix(x2, row(norm_mix_pre), w_in_b, l, conf_dw_w[l], row(conf_dw_b),
                                       row(conf_ln_g), row(conf_ln_b), sc_conv_w[l], tm_in, t)
        pf3 = pf.reshape(b, t, N_PF * gw)
        qkv3 = qkv.reshape(b, t, 3 * gw)
        y_sb = _stickbreak(qkv3, tk)
        y_ret = _retention(pf3, tables, row(ret_norm_g), tt_ret)
        ys = [y_conf, y_sb.reshape(n, gw), y_ret.reshape(n, gw), y_sc]
        x2 = _outproj(ys, w_out_b, l, row(norm_mix_post), x2, tm_out)
        x2 = _ffn(x2, row(norm_ffn_pre), row(norm_ffn_post), wup_b, ffn_conv_w, wdown_b, l, tm_ffn, t)
    return x2.reshape(b, t, d)
```
